```python
import math
import jax, jax.numpy as jnp
from jax import lax
import numpy as np

D_MODEL = 2048
BATCH = 2
SEQ = 8192
DEPTH = 4
DEC_BATCH = 8
DEC_SEQ = 16
PAST_LEN = 4096

CHUNK = 64
N_MIXERS = 2
N_GDN = (DEPTH + 1) // 2
N_SB = DEPTH // 2
EPS = 1e-6
GDN_HK = 16
GDN_HV = 32
GDN_DK = 128
GDN_DV = 128
CONV_W = 4
QK_DIM = GDN_HK * GDN_DK
V_DIM = GDN_HV * GDN_DV
QKV_DIM = 2 * QK_DIM + V_DIM
GDN_IN = QKV_DIM + V_DIM + 2 * GDN_HV
SB_HEADS = 16
SB_HEAD_DIM = D_MODEL // SB_HEADS
SB_BLOCK = 128
D_FF = 5632
N_EXPERTS = 8
TOP_K = 2
D_FF_EXPERT = 7168

kernel_name = "hybrid_gdn_stickbreak_stream_step"


def rmsnorm(x, w):
    xf = x.astype(jnp.float32)
    y = xf * lax.rsqrt(jnp.mean(xf * xf, axis=-1, keepdims=True) + EPS)
    return (y * w.astype(jnp.float32)).astype(x.dtype)


def l2norm(x):
    xf = x.astype(jnp.float32)
    return xf * lax.rsqrt(jnp.sum(xf * xf, axis=-1, keepdims=True) + EPS)


def swiglu(h, w_gate, w_up, w_down):
    return (jax.nn.silu(h @ w_gate) * (h @ w_up)) @ w_down


def moe_swiglu(h, router, w_gate, w_up, w_down):
    logits = (h @ router).astype(jnp.float32)
    top_v, top_i = lax.top_k(logits, TOP_K)
    top_w = jax.nn.softmax(top_v, axis=-1)
    gates = jnp.sum(jax.nn.one_hot(top_i, N_EXPERTS, dtype=jnp.float32) * top_w[..., None], axis=-2)
    gates = gates.astype(h.dtype)
    out = jnp.zeros_like(h)
    for e in range(N_EXPERTS):
        out = out + gates[..., e:e + 1] * swiglu(h, w_gate[e], w_up[e], w_down[e])
    return out


def causal_depthwise_conv(x, buf, w):
    L = x.shape[1]
    xp = jnp.concatenate([buf.astype(x.dtype), x], axis=1)
    y = xp[:, 0:L, :] * w[0]
    for j in range(1, CONV_W):
        y = y + xp[:, j:j + L, :] * w[j]
    return y, xp[:, L:, :]


def chunked_gated_delta(q, k, v, g, beta, S0):
    B, L, H, _ = q.shape
    out_dtype = v.dtype
    C = CHUNK if L >= CHUNK else L
    N = L // C
    f32 = jnp.float32

    def to_chunks(t):
        return t.reshape((B, N, C) + t.shape[2:]).swapaxes(2, 3).astype(f32)

    q, k, v, g, beta = to_chunks(q), to_chunks(k), to_chunks(v), to_chunks(g), to_chunks(beta)
    gc = jnp.cumsum(g, axis=-1)
    idx = jnp.arange(C)
    causal = idx[:, None] >= idx[None, :]
    strict = idx[:, None] > idx[None, :]
    decay = jnp.exp(jnp.where(causal, gc[..., :, None] - gc[..., None, :], -jnp.inf))
    kb = k * beta[..., None]
    A = jnp.where(strict, jnp.einsum('bnhid,bnhjd->bnhij', kb, k) * decay, 0.0)
    rhs = jnp.concatenate([v * beta[..., None], kb * jnp.exp(gc)[..., None]], axis=-1)
    sol = lax.linalg.triangular_solve(A + jnp.eye(C, dtype=f32), rhs, left_side=True, lower=True,
                                      unit_diagonal=True)
    u, w = sol[..., :GDN_DV], sol[..., GDN_DV:]
    attn = jnp.einsum('bnhid,bnhjd->bnhij', q, k) * decay
    q_dec = q * jnp.exp(gc)[..., None]
    k_dec = k * jnp.exp(gc[..., -1:] - gc)[..., None]
    g_last = jnp.exp(gc[..., -1])

    def step(S, xs):
        u_n, w_n, qd_n, kd_n, at_n, gl_n = xs
        v_new = u_n - jnp.einsum('bhck,bhkv->bhcv', w_n, S)
        o_n = jnp.einsum('bhck,bhkv->bhcv', qd_n, S) + jnp.einsum('bhij,bhjv->bhiv', at_n, v_new)
        S = S * gl_n[..., None, None] + jnp.einsum('bhck,bhcv->bhkv', kd_n, v_new)
        return S, o_n

    xs = tuple(t.swapaxes(0, 1) for t in (u, w, q_dec, k_dec, attn, g_last))
    S, o = lax.scan(step, S0.astype(f32), xs)
    o = o.transpose(1, 0, 3, 2, 4).reshape(B, L, H, GDN_DV)
    return o.astype(out_dtype), S.astype(S0.dtype)


def gated_deltanet(h, conv_buf, S0, w_in, conv_w, a_log, dt_bias, norm_w, w_out):
    B, L, _ = h.shape
    proj = h @ w_in
    qkv = proj[..., :QKV_DIM]
    z = proj[..., QKV_DIM:QKV_DIM + V_DIM]
    b = proj[..., QKV_DIM + V_DIM:QKV_DIM + V_DIM + GDN_HV]
    a = proj[..., QKV_DIM + V_DIM + GDN_HV:]
    qkv, new_buf = causal_depthwise_conv(qkv, conv_buf, conv_w)
    qkv = jax.nn.silu(qkv)
    rep = GDN_HV // GDN_HK
    q = l2norm(qkv[..., :QK_DIM].reshape(B, L, GDN_HK, GDN_DK)) * (GDN_DK ** -0.5)
    k = l2norm(qkv[..., QK_DIM:2 * QK_DIM].reshape(B, L, GDN_HK, GDN_DK))
    q = jnp.repeat(q, rep, axis=2)
    k = jnp.repeat(k, rep, axis=2)
    v = qkv[..., 2 * QK_DIM:].reshape(B, L, GDN_HV, GDN_DV)
    beta = jax.nn.sigmoid(b.astype(jnp.float32))
    g = -jnp.exp(a_log.astype(jnp.float32)) * jax.nn.softplus(a.astype(jnp.float32) + dt_bias.astype(jnp.float32))
    o, S = chunked_gated_delta(q, k, v, g, beta, S0)
    o = rmsnorm(o, norm_w) * jax.nn.silu(z.reshape(B, L, GDN_HV, GDN_DV))
    return o.reshape(B, L, V_DIM) @ w_out, new_buf, S


def stick_breaking_core(q, k, v, q_pos0):
    B, Lq, H, D = q.shape
    Lk = k.shape[1]
    blk = min(SB_BLOCK, Lq)
    nb = Lq // blk
    qb = q.reshape(B, nb, blk, H, D).swapaxes(0, 1)
    kpos = jnp.arange(Lk)
    scale = D ** -0.5

    def one_block(args):
        qi, i = args
        qpos = q_pos0 + i * blk + jnp.arange(blk)
        z = jnp.einsum('bqhd,bkhd->bhqk', qi, k).astype(jnp.float32) * scale
        mask = kpos[None, :] < qpos[:, None]
        log_beta = jnp.where(mask, jax.nn.log_sigmoid(z), -jnp.inf)
        log_rest = jnp.where(mask, jax.nn.log_sigmoid(-z), 0.0)
        after = lax.cumsum(log_rest, axis=log_rest.ndim - 1, reverse=True) - log_rest
        wts = jnp.exp(log_beta + after)
        return jnp.einsum('bhqk,bkhd->bqhd', wts.astype(v.dtype), v)

    out = lax.map(one_block, (qb, jnp.arange(nb)))
    return out.swapaxes(0, 1).reshape(B, Lq, H, D)


def stick_breaking_attention(h, k_past, v_past, w_qkv, w_out):
    B, L, _ = h.shape
    qkv = (h @ w_qkv).reshape(B, L, 3, SB_HEADS, SB_HEAD_DIM)
    q, k, v = qkv[:, :, 0], qkv[:, :, 1], qkv[:, :, 2]
    k_all = jnp.concatenate([k_past.astype(k.dtype), k], axis=1)
    v_all = jnp.concatenate([v_past.astype(v.dtype), v], axis=1)
    o = stick_breaking_core(q, k_all, v_all, k_past.shape[1])
    return o.reshape(B, L, SB_HEADS * SB_HEAD_DIM) @ w_out, k, v


def trunk(x, gdn_S, gdn_conv, sb_k_past, sb_v_past,
          norm_mix, norm_ffn, norm_final,
          gdn_w_in, gdn_conv_w, gdn_a_log, gdn_dt_bias, gdn_norm_w, gdn_w_out,
          sb_w_qkv, sb_w_out,
          ffn_w_gate, ffn_w_up, ffn_w_down,
          moe_router, moe_w_gate, moe_w_up, moe_w_down):
    new_S, new_conv, new_k, new_v = [], [], [], []
    for i in range(DEPTH):
        j = i // N_MIXERS
        h = rmsnorm(x, norm_mix[i])
        if i % N_MIXERS == 0:
            y, c, s = gated_deltanet(h, gdn_conv[j], gdn_S[j], gdn_w_in[j], gdn_conv_w[j], gdn_a_log[j],
                                     gdn_dt_bias[j], gdn_norm_w[j], gdn_w_out[j])
            new_conv.append(c)
            new_S.append(s)
        else:
            y, kn, vn = stick_breaking_attention(h, sb_k_past[j], sb_v_past[j], sb_w_qkv[j], sb_w_out[j])
            new_k.append(kn)
            new_v.append(vn)
        x = x + y
        h = rmsnorm(x, norm_ffn[i])
        if i % 2 == 0:
            x = x + swiglu(h, ffn_w_gate[j], ffn_w_up[j], ffn_w_down[j])
        else:
            x = x + moe_swiglu(h, moe_router[j], moe_w_gate[j], moe_w_up[j], moe_w_down[j])
    return rmsnorm(x, norm_final), jnp.stack(new_S), jnp.stack(new_conv), jnp.stack(new_k), jnp.stack(new_v)


def setup_inputs(seed: int = 0) -> dict:
    key = jax.random.key(seed)
    ks = jax.random.split(key, 24)
    f32 = jnp.float32

    def nrm(k, shape, scale):
        return jax.random.normal(k, shape, f32) * scale

    def gain(k, shape):
        return 1.0 + 0.02 * jax.random.normal(k, shape, f32)

    dt = jnp.exp(jax.random.uniform(ks[12], (N_GDN, GDN_HV), f32, math.log(1e-3), math.log(1e-1)))
    return {
        "x_prompt": nrm(ks[0], (BATCH, SEQ, D_MODEL), 1.0),
        "x_sample": nrm(ks[1], (DEC_BATCH, DEC_SEQ, D_MODEL), 1.0),
        "state_gdn_S": nrm(ks[2], (N_GDN, DEC_BATCH, GDN_HV, GDN_DK, GDN_DV), 0.1),
        "state_gdn_conv": nrm(ks[3], (N_GDN, DEC_BATCH, CONV_W - 1, QKV_DIM), 1.0),
        "cache_sb_k": nrm(ks[4], (N_SB, DEC_BATCH, PAST_LEN, SB_HEADS, SB_HEAD_DIM), 1.0),
        "cache_sb_v": nrm(ks[5], (N_SB, DEC_BATCH, PAST_LEN, SB_HEADS, SB_HEAD_DIM), 1.0),
        "norm_mix": gain(ks[6], (DEPTH, D_MODEL)),
        "norm_ffn": gain(ks[7], (DEPTH, D_MODEL)),
        "norm_final": gain(ks[8], (D_MODEL,)),
        "gdn_w_in": nrm(ks[9], (N_GDN, D_MODEL, GDN_IN), D_MODEL ** -0.5),
        "gdn_conv_w": nrm(ks[10], (N_GDN, CONV_W, QKV_DIM), CONV_W ** -0.5),
        "gdn_a_log": jnp.log(jax.random.uniform(ks[11], (N_GDN, GDN_HV), f32, 1.0, 16.0)),
        "gdn_dt_bias": dt + jnp.log(-jnp.expm1(-dt)),
        "gdn_norm_w": gain(ks[13], (N_GDN, GDN_DV)),
        "gdn_w_out": nrm(ks[14], (N_GDN, V_DIM, D_MODEL), V_DIM ** -0.5),
        "sb_w_qkv": nrm(ks[15], (N_SB, D_MODEL, 3 * SB_HEADS * SB_HEAD_DIM), D_MODEL ** -0.5),
        "sb_w_out": nrm(ks[16], (N_SB, SB_HEADS * SB_HEAD_DIM, D_MODEL), (SB_HEADS * SB_HEAD_DIM) ** -0.5),
        "ffn_w_gate": nrm(ks[17], (N_GDN, D_MODEL, D_FF), D_MODEL ** -0.5),
        "ffn_w_up": nrm(ks[18], (N_GDN, D_MODEL, D_FF), D_MODEL ** -0.5),
        "ffn_w_down": nrm(ks[19], (N_GDN, D_FF, D_MODEL), D_FF ** -0.5),
        "moe_router": nrm(ks[20], (N_SB, D_MODEL, N_EXPERTS), D_MODEL ** -0.5),
        "moe_w_gate": nrm(ks[21], (N_SB, N_EXPERTS, D_MODEL, D_FF_EXPERT), D_MODEL ** -0.5),
        "moe_w_up": nrm(ks[22], (N_SB, N_EXPERTS, D_MODEL, D_FF_EXPERT), D_MODEL ** -0.5),
        "moe_w_down": nrm(ks[23], (N_SB, N_EXPERTS, D_FF_EXPERT, D_MODEL), D_FF_EXPERT ** -0.5),
    }


def reference(x_prompt, x_sample, state_gdn_S, state_gdn_conv, cache_sb_k, cache_sb_v,
              norm_mix, norm_ffn, norm_final,
              gdn_w_in, gdn_conv_w, gdn_a_log, gdn_dt_bias, gdn_norm_w, gdn_w_out,
              sb_w_qkv, sb_w_out,
              ffn_w_gate, ffn_w_up, ffn_w_down,
              moe_router, moe_w_gate, moe_w_up, moe_w_down):
    params = (norm_mix, norm_ffn, norm_final,
              gdn_w_in, gdn_conv_w, gdn_a_log, gdn_dt_bias, gdn_norm_w, gdn_w_out,
              sb_w_qkv, sb_w_out,
              ffn_w_gate, ffn_w_up, ffn_w_down,
              moe_router, moe_w_gate, moe_w_up, moe_w_down)
    bp = x_prompt.shape[0]
    dt = x_prompt.dtype
    zero_S = jnp.zeros((N_GDN, bp, GDN_HV, GDN_DK, GDN_DV), dt)
    zero_conv = jnp.zeros((N_GDN, bp, CONV_W - 1, QKV_DIM), dt)
    empty_kv = jnp.zeros((N_SB, bp, 0, SB_HEADS, SB_HEAD_DIM), dt)
    y_prompt, S_p, conv_p, k_p, v_p = trunk(x_prompt, zero_S, zero_conv, empty_kv, empty_kv, *params)
    y_sample, S_s, conv_s, k_s, v_s = trunk(x_sample, state_gdn_S, state_gdn_conv, cache_sb_k, cache_sb_v, *params)
    return (y_prompt, y_sample, S_p, conv_p, k_p, v_p, S_s, conv_s, k_s, v_s)
```

```python
import functools

import jax
import jax.numpy as jnp
from jax import lax
from jax.experimental import pallas as pl
from jax.experimental.pallas import tpu as pltpu

F32 = jnp.float32
BF16 = jnp.bfloat16
I32 = jnp.int32

EPS = 1e-6
LANE = 128
SUBLANE = 8
CHUNK = 64
CONV_W = 4
TOP_K = 2
SB_SUB = LANE
SB_TILE = 512
GDN_GROUP = 4
DEC_LANES = 64
MOE_TILE = 512
VMEM_LIMIT = 52 * 1024 * 1024
MATMUL_VMEM_BUDGET = 36 * 1024 * 1024


def _cp(sem, vmem=VMEM_LIMIT):
    return pltpu.CompilerParams(dimension_semantics=sem, vmem_limit_bytes=vmem)


def _row_tile(n, cap, mult=16):
    best = None
    for d in range(mult, min(n, cap) + 1, mult):
        if n % d == 0:
            best = d
    assert best is not None, (n, cap)
    return best


def _col_tile(n, cap):
    return _row_tile(n, cap, LANE)


def _sigmoid(x):
    return 1.0 / (1.0 + jnp.exp(-x))


def _dotf(a, b):
    return jnp.dot(a, b, preferred_element_type=F32)


def _dot_nt(a, b):
    return lax.dot_general(a, b, (((1,), (1,)), ((), ())), preferred_element_type=F32)


def _dot_tn(a, b):
    return lax.dot_general(a, b, (((0,), (0,)), ((), ())), preferred_element_type=F32)


def _split3(a):
    a1 = a.astype(BF16)
    r = a - a1.astype(F32)
    a2 = r.astype(BF16)
    a3 = (r - a2.astype(F32)).astype(BF16)
    return a1, a2, a3


def _dot_exact_rhs(a, b_bf16):
    a1, a2, a3 = _split3(a)
    return _dotf(a1, b_bf16) + _dotf(a2, b_bf16) + _dotf(a3, b_bf16)


def _dot_exact_lhs(a_bf16, b):
    b1, b2, b3 = _split3(b)
    return _dotf(a_bf16, b1) + _dotf(a_bf16, b2) + _dotf(a_bf16, b3)


def _rmsnorm_kernel(x_ref, w_ref, h_ref):
    x = x_ref[...]
    ms = jnp.mean(x * x, axis=-1, keepdims=True)
    h_ref[...] = (x * lax.rsqrt(ms + EPS) * w_ref[...]).astype(h_ref.dtype)


def rmsnorm(x, w, out_dtype):
    t, d = x.shape
    tm = _row_tile(t, 688)
    return pl.pallas_call(
        _rmsnorm_kernel,
        grid=(t // tm,),
        in_specs=[pl.BlockSpec((tm, d), lambda i: (i, 0)), pl.BlockSpec((1, d), lambda i: (0, 0))],
        out_specs=pl.BlockSpec((tm, d), lambda i: (i, 0)),
        out_shape=jax.ShapeDtypeStruct((t, d), out_dtype),
        compiler_params=_cp(("parallel",)),
        name="rmsnorm",
    )(x, w.reshape(1, d))


def _norm_route_kernel(x_ref, w_ref, r_ref, h_ref, route_ref, *, n_experts):
    x = x_ref[...]
    ms = jnp.mean(x * x, axis=-1, keepdims=True)
    h = x * lax.rsqrt(ms + EPS) * w_ref[...]
    h_ref[...] = h.astype(h_ref.dtype)
    h1, h2, h3 = _split3(h)
    r1, r2, r3 = _split3(r_ref[...])
    logits = (_dotf(h1, r1) + (_dotf(h1, r2) + _dotf(h2, r1))
              + (_dotf(h2, r2) + _dotf(h1, r3) + _dotf(h3, r1)))
    lane = lax.broadcasted_iota(I32, logits.shape, 1).astype(F32)
    neg = -jnp.inf
    l1 = jnp.where(lane < n_experts, logits, neg)
    m1 = jnp.max(l1, axis=-1, keepdims=True)
    i1 = jnp.min(jnp.where(l1 == m1, lane, float(LANE)), axis=-1, keepdims=True)
    l2 = jnp.where(lane == i1, neg, l1)
    m2 = jnp.max(l2, axis=-1, keepdims=True)
    i2 = jnp.min(jnp.where(l2 == m2, lane, float(LANE)), axis=-1, keepdims=True)
    e2 = jnp.exp(m2 - m1)
    w1 = 1.0 / (1.0 + e2)
    w2 = e2 / (1.0 + e2)
    route_ref[...] = jnp.where(lane == 0, i1, jnp.where(lane == 1, i2, jnp.where(lane == 2, w1, jnp.where(lane == 3, w2, 0.0))))


def norm_route(x, w, router):
    t, d = x.shape
    n_experts = router.shape[1]
    tm = _row_tile(t, 688)
    r_pad = jnp.zeros((d, LANE), F32).at[:, :n_experts].set(router)
    return pl.pallas_call(
        functools.partial(_norm_route_kernel, n_experts=n_experts),
        grid=(t // tm,),
        in_specs=[pl.BlockSpec((tm, d), lambda i: (i, 0)), pl.BlockSpec((1, d), lambda i: (0, 0)),
                  pl.BlockSpec((d, LANE), lambda i: (0, 0))],
        out_specs=[pl.BlockSpec((tm, d), lambda i: (i, 0)), pl.BlockSpec((tm, LANE), lambda i: (i, 0))],
        out_shape=[jax.ShapeDtypeStruct((t, d), BF16), jax.ShapeDtypeStruct((t, LANE), F32)],
        compiler_params=_cp(("parallel",)),
        name="norm_route",
    )(x, w.reshape(1, d), r_pad)


def _mm_kernel(x_ref, w_ref, o_ref):
    o_ref[...] = _dotf(x_ref[...], w_ref[...]).astype(o_ref.dtype)


def _mm_res_kernel(x_ref, w_ref, r_ref, o_ref):
    o_ref[...] = (r_ref[...] + _dotf(x_ref[...], w_ref[...])).astype(o_ref.dtype)


def matmul(x, w, resid=None, out_dtype=F32):
    t, k = x.shape
    n = w.shape[1]
    out_bytes = 4 * (3 if resid is None else 5)

    def fits(tm, tn):
        return 4 * k * (tm + tn) + out_bytes * tm * tn <= MATMUL_VMEM_BUDGET

    tm, tn = max(((a, b) for a in (_row_tile(t, c) for c in (1376, 688, 384, 128)) for b in (_col_tile(n, c) for c in (1024, 512, 256, 128))
                  if fits(a, b)), key=lambda ab: ab[0] * ab[1])
    in_specs = [pl.BlockSpec((tm, k), lambda j, i: (i, 0)), pl.BlockSpec((k, tn), lambda j, i: (0, j))]
    args = [x, w]
    kern = _mm_kernel
    if resid is not None:
        in_specs.append(pl.BlockSpec((tm, tn), lambda j, i: (i, j)))
        args.append(resid)
        kern = _mm_res_kernel
    return pl.pallas_call(
        kern,
        grid=(n // tn, t // tm),
        in_specs=in_specs,
        out_specs=pl.BlockSpec((tm, tn), lambda j, i: (i, j)),
        out_shape=jax.ShapeDtypeStruct((t, n), out_dtype),
        compiler_params=_cp(("parallel", "parallel")),
        name="matmul",
    )(*args)


def _swiglu_step(x_ref, wg_ref, wu_ref, wd_ref):
    x = x_ref[...]
    g = _dotf(x, wg_ref[...])
    u = _dotf(x, wu_ref[...])
    a = (g * _sigmoid(g) * u).astype(BF16)
    return _dotf(a, wd_ref[...])


def _ffn_kernel(x_ref, wg_ref, wu_ref, wd_ref, r_ref, o_ref):
    f = pl.program_id(1)
    y = _swiglu_step(x_ref, wg_ref, wu_ref, wd_ref)

    @pl.when(f == 0)
    def _():
        o_ref[...] = r_ref[...] + y

    @pl.when(f > 0)
    def _():
        o_ref[...] += y


def ffn(h, wg, wu, wd, resid):
    t, d = h.shape
    dff = wg.shape[1]
    tm = _row_tile(t, 688)
    tf = _col_tile(dff, 512)
    return pl.pallas_call(
        _ffn_kernel,
        grid=(t // tm, dff // tf),
        in_specs=[pl.BlockSpec((tm, d), lambda i, f: (i, 0)),
                  pl.BlockSpec((d, tf), lambda i, f: (0, f)),
                  pl.BlockSpec((d, tf), lambda i, f: (0, f)),
                  pl.BlockSpec((tf, d), lambda i, f: (f, 0)),
                  pl.BlockSpec((tm, d), lambda i, f: (i, 0))],
        out_specs=pl.BlockSpec((tm, d), lambda i, f: (i, 0)),
        out_shape=jax.ShapeDtypeStruct((t, d), F32),
        compiler_params=_cp(("parallel", "arbitrary")),
        name="ffn",
    )(h, wg, wu, wd, resid)


def _moe_ffn_kernel(te_ref, nu_ref, x_ref, wg_ref, wu_ref, wd_ref, o_ref):
    i = pl.program_id(0)
    f = pl.program_id(1)

    @pl.when(i < nu_ref[0])
    def _():
        y = _swiglu_step(x_ref, wg_ref, wu_ref, wd_ref)

        @pl.when(f == 0)
        def _():
            o_ref[...] = y

        @pl.when(f > 0)
        def _():
            o_ref[...] += y


def moe_ffn(xs, tile_expert, n_used, wg, wu, wd):
    p, d = xs.shape
    dff = wg.shape[2]
    tm = MOE_TILE
    tf = _col_tile(dff, 512)
    nf = dff // tf

    def fidx(i, f, nu):
        return jnp.where(i < nu[0], f, nf - 1)

    grid_spec = pltpu.PrefetchScalarGridSpec(
        num_scalar_prefetch=2,
        grid=(p // tm, nf),
        in_specs=[pl.BlockSpec((tm, d), lambda i, f, te, nu: (jnp.minimum(i, nu[0] - 1), 0)),
                  pl.BlockSpec((None, d, tf), lambda i, f, te, nu: (te[i], 0, fidx(i, f, nu))),
                  pl.BlockSpec((None, d, tf), lambda i, f, te, nu: (te[i], 0, fidx(i, f, nu))),
                  pl.BlockSpec((None, tf, d), lambda i, f, te, nu: (te[i], fidx(i, f, nu), 0))],
        out_specs=pl.BlockSpec((tm, d), lambda i, f, te, nu: (i, 0)),
    )
    return pl.pallas_call(
        _moe_ffn_kernel,
        grid_spec=grid_spec,
        out_shape=jax.ShapeDtypeStruct((p, d), F32),
        compiler_params=_cp(("arbitrary", "arbitrary")),
        name="moe_ffn",
    )(tile_expert, n_used, xs, wg, wu, wd)


def _gdn_gate_kernel(ba_ref, alog_ref, dtb_ref, ltri_ref, exp_ref, scal_ref, dec_ref, *, group, chunk):
    g4 = group
    ba = ba_ref[...]
    tm, width = ba.shape
    lane = lax.broadcasted_iota(I32, ba.shape, 1) % LANE
    is_b = lane < g4
    is_a = jnp.logical_and(lane >= g4, lane < 2 * g4)
    beta = _sigmoid(ba)
    xa = ba + dtb_ref[...]
    softplus = jnp.maximum(xa, 0.0) + jnp.log1p(jnp.exp(-jnp.abs(xa)))
    g = jnp.where(is_a, -jnp.exp(alog_ref[...]) * softplus, 0.0)
    gc = _dot_exact_lhs(ltri_ref[...], g)
    gc3 = gc.reshape(tm // chunk, chunk, width)
    gl = jnp.broadcast_to(gc3[:, chunk - 1:chunk, :], gc3.shape).reshape(tm, width)
    scal = (jnp.where(is_b, beta, 0.0) + jnp.where(is_a, gc, 0.0)
            + pltpu.roll(jnp.where(is_a, gl, 0.0), g4, 1)
            + pltpu.roll(jnp.where(is_a, gl - gc, 0.0), 2 * g4, 1))
    scal_ref[...] = scal

    dw = g4 * DEC_LANES
    i_idx = lax.broadcasted_iota(I32, (tm, dw), 0) % chunk
    j_idx = lax.broadcasted_iota(I32, (tm, dw), 1) % DEC_LANES
    for nb in range(width // LANE):
        gce = _dot_exact_rhs(gc[:, nb * LANE:(nb + 1) * LANE], exp_ref[...])
        diag = jnp.where(i_idx == j_idx, gce, 0.0).reshape(tm // chunk, chunk, dw)
        gcj = jnp.broadcast_to(jnp.sum(diag, axis=1, keepdims=True), diag.shape).reshape(tm, dw)
        dec = jnp.where(i_idx >= j_idx, jnp.exp(gce - gcj), 0.0)
        dec_ref[:, nb * dw:(nb + 1) * dw] = dec


def gdn_gates(ba, alog2, dtb2, chunk):
    rows, width = ba.shape
    tm = _row_tile(rows, 512, chunk)
    ng = width // LANE
    g4 = GDN_GROUP
    r = jnp.arange(tm)
    ltri = jnp.logical_and(r[:, None] >= r[None, :], (r[:, None] // chunk) == (r[None, :] // chunk)).astype(BF16)
    src = jnp.arange(LANE)[:, None]
    dst = jnp.arange(g4 * DEC_LANES)[None, :]
    expand = (src == g4 + dst // DEC_LANES).astype(BF16)
    return pl.pallas_call(
        functools.partial(_gdn_gate_kernel, group=g4, chunk=chunk),
        grid=(rows // tm,),
        in_specs=[pl.BlockSpec((tm, width), lambda i: (i, 0)),
                  pl.BlockSpec((1, width), lambda i: (0, 0)),
                  pl.BlockSpec((1, width), lambda i: (0, 0)),
                  pl.BlockSpec((tm, tm), lambda i: (0, 0)),
                  pl.BlockSpec((LANE, g4 * DEC_LANES), lambda i: (0, 0))],
        out_specs=[pl.BlockSpec((tm, width), lambda i: (i, 0)),
                   pl.BlockSpec((tm, ng * g4 * DEC_LANES), lambda i: (i, 0))],
        out_shape=[jax.ShapeDtypeStruct((rows, width), F32),
                   jax.ShapeDtypeStruct((rows, ng * g4 * DEC_LANES), F32)],
        compiler_params=_cp(("parallel",)),
        name="gdn_gates",
    )(ba, alog2, dtb2, ltri, expand)


def _gdn_conv_kernel(q_ref, k_ref, v_ref, qh_ref, kh_ref, vh_ref, qs_ref, ks_ref, vs_ref,
                     wq_ref, wk_ref, wv_ref, scal_ref,
                     qo_ref, ko_ref, kbo_ref, vbo_ref, kbgo_ref, qdo_ref, kdo_ref, *, group, rep, tm, q_scale):
    first = pl.program_id(1) == 0

    def conv_silu(x_ref, halo_ref, hist_ref, w_ref):
        halo = jnp.where(first, hist_ref[...], halo_ref[...])
        xp = jnp.concatenate([halo, x_ref[...]], axis=0)
        w = w_ref[...]
        base = SUBLANE - (CONV_W - 1)
        y = xp[base:base + tm] * w[0:1]
        for j in range(1, CONV_W):
            y = y + xp[base + j:base + j + tm] * w[j:j + 1]
        return y * _sigmoid(y)

    def l2norm_heads(y):
        outs = []
        for h in range(y.shape[1] // LANE):
            yh = y[:, h * LANE:(h + 1) * LANE]
            outs.append(yh * lax.rsqrt(jnp.sum(yh * yh, axis=-1, keepdims=True) + EPS))
        return outs

    qn = [qh * q_scale for qh in l2norm_heads(conv_silu(q_ref, qh_ref, qs_ref, wq_ref))]
    kn = l2norm_heads(conv_silu(k_ref, kh_ref, ks_ref, wk_ref))
    v = conv_silu(v_ref, vh_ref, vs_ref, wv_ref)
    scal = scal_ref[...]
    for hq in range(group // rep):
        sl = slice(hq * LANE, (hq + 1) * LANE)
        qo_ref[:, sl] = qn[hq].astype(BF16)
        ko_ref[:, sl] = kn[hq].astype(BF16)
    for gh in range(group):
        hq = gh // rep
        sl = slice(gh * LANE, (gh + 1) * LANE)
        beta = scal[:, gh:gh + 1]
        eg = jnp.exp(scal[:, group + gh:group + gh + 1])
        ek = jnp.exp(scal[:, 3 * group + gh:3 * group + gh + 1])
        kb = kn[hq] * beta
        kbo_ref[:, sl] = kb.astype(BF16)
        vbo_ref[:, sl] = (v[:, sl] * beta).astype(BF16)
        kbgo_ref[:, sl] = (kb * eg).astype(BF16)
        qdo_ref[:, sl] = (qn[hq] * eg).astype(BF16)
        kdo_ref[:, sl] = (kn[hq] * ek).astype(BF16)


def gdn_conv(proj, row0, nseq, seqlen, hist, conv_w, scal, qk_dim, v_dim, dk):
    g4 = GDN_GROUP
    hv = v_dim // LANE
    rep = hv // (qk_dim // LANE)
    ng = hv // g4
    wqk = (g4 // rep) * LANE
    wv = g4 * LANE
    tm = _row_tile(seqlen, 256)
    nl = seqlen // tm
    rows = nseq * seqlen
    rb0 = row0 // tm
    nqk = qk_dim // wqk
    nv0 = 2 * qk_dim // wv
    cw = jnp.zeros((SUBLANE, conv_w.shape[1]), F32).at[:CONV_W].set(conv_w)

    def main(w, c0):
        return pl.BlockSpec((tm, w), lambda s, i, j: (rb0 + s * nl + i, c0 + j))

    def halo(w, c0):
        return pl.BlockSpec((SUBLANE, w), lambda s, i, j: (jnp.maximum((row0 + s * seqlen + i * tm) // SUBLANE - 1, 0), c0 + j))

    def hst(w, c0):
        return pl.BlockSpec((None, SUBLANE, w), lambda s, i, j: (s, 0, c0 + j))

    def wts(w, c0):
        return pl.BlockSpec((SUBLANE, w), lambda s, i, j: (0, c0 + j))

    def out(w):
        return pl.BlockSpec((tm, w), lambda s, i, j: (s * nl + i, j))

    return pl.pallas_call(
        functools.partial(_gdn_conv_kernel, group=g4, rep=rep, tm=tm, q_scale=float(dk) ** -0.5),
        grid=(nseq, nl, ng),
        in_specs=[main(wqk, 0), main(wqk, nqk), main(wv, nv0),
                  halo(wqk, 0), halo(wqk, nqk), halo(wv, nv0),
                  hst(wqk, 0), hst(wqk, nqk), hst(wv, nv0),
                  wts(wqk, 0), wts(wqk, nqk), wts(wv, nv0),
                  pl.BlockSpec((tm, LANE), lambda s, i, j: (s * nl + i, j))],
        out_specs=[out(wqk), out(wqk), out(wv), out(wv), out(wv), out(wv), out(wv)],
        out_shape=[jax.ShapeDtypeStruct((rows, qk_dim), BF16)] * 2 + [jax.ShapeDtypeStruct((rows, v_dim), BF16)] * 5,
        compiler_params=_cp(("parallel", "parallel", "parallel")),
        name="gdn_conv",
    )(proj, proj, proj, proj, proj, proj, hist, hist, hist, cw, cw, cw, scal)


def _gdn_kernel(q_ref, k_ref, kb_ref, vb_ref, kbg_ref, qd_ref, kd_ref, dec_ref, scal_ref, z_ref, nw_ref, s0_ref,
                oprev_ref, o_ref, sout_ref, s_scr, *, group, rep, chunk, nchunks):
    del oprev_ref
    l = pl.program_id(2)

    @pl.when(l == 0)
    def _():
        s_scr[...] = s0_ref[...]

    ii = lax.broadcasted_iota(I32, (chunk, chunk), 0)
    jj = lax.broadcasted_iota(I32, (chunk, chunk), 1)
    strict = ii > jj
    eye = (ii == jj).astype(F32)
    nw = nw_ref[...]

    def one_chunk(c, carry):
        r0 = pl.multiple_of(c * chunk, chunk)
        rows = pl.ds(r0, chunk)
        sc = scal_ref[rows, :]
        for gh in range(group):
            hq = gh // rep
            slq = slice(hq * LANE, (hq + 1) * LANE)
            sl = slice(gh * LANE, (gh + 1) * LANE)
            k = k_ref[rows, slq]
            q = q_ref[rows, slq]
            dm = dec_ref[rows, gh * DEC_LANES:gh * DEC_LANES + chunk]
            a = jnp.where(strict, _dot_nt(kb_ref[rows, sl], k) * dm, 0.0)
            attn = _dot_nt(q, k) * dm
            p = eye - a
            am = a.astype(BF16)
            pw = 1
            while 2 * pw < chunk:
                am = _dotf(am, am).astype(BF16)
                p = p + _dotf(p.astype(BF16), am)
                pw *= 2
            rhs = jnp.concatenate([vb_ref[rows, sl], kbg_ref[rows, sl]], axis=1)
            sol = _dotf(p.astype(BF16), rhs)
            u = sol[:, :LANE]
            w = sol[:, LANE:]
            s = s_scr[gh]
            sb = s.astype(BF16)
            vn = u - _dotf(w.astype(BF16), sb)
            vnb = vn.astype(BF16)
            o = _dotf(qd_ref[rows, sl], sb) + _dotf(attn.astype(BF16), vnb)
            egl = jnp.exp(sc[0:1, 2 * group + gh:2 * group + gh + 1])
            s_scr[gh] = s * egl + _dot_tn(kd_ref[rows, sl], vnb)
            ms = jnp.mean(o * o, axis=-1, keepdims=True)
            zz = z_ref[rows, sl]
            o_ref[rows, sl] = (o * lax.rsqrt(ms + EPS) * nw * (zz * _sigmoid(zz))).astype(o_ref.dtype)
        return carry

    lax.fori_loop(0, nchunks, one_chunk, 0)

    @pl.when(l == pl.num_programs(2) - 1)
    def _():
        sout_ref[...] = s_scr[...]


def gdn_core(pre, dec, scal, proj, row0, nseq, seqlen, chunk, s0, norm_w, z_col0, o_all):
    q, k, kb, vb, kbg, qd, kd = pre
    g4 = GDN_GROUP
    v_dim = kb.shape[1]
    hv = v_dim // LANE
    rep = hv // (q.shape[1] // LANE)
    ng = hv // g4
    wqk = (g4 // rep) * LANE
    wv = g4 * LANE
    lc = _row_tile(seqlen, 512, chunk)
    nl = seqlen // lc
    zc0 = z_col0 // wv
    rb0 = row0 // lc

    def loc(w):
        return pl.BlockSpec((lc, w), lambda s, h, l: (s * nl + l, h))

    def glob(w, c0):
        return pl.BlockSpec((lc, w), lambda s, h, l: (rb0 + s * nl + l, c0 + h))

    state = pl.BlockSpec((None, g4, LANE, LANE), lambda s, h, l: (s, h, 0, 0))
    o_new, s_out = pl.pallas_call(
        functools.partial(_gdn_kernel, group=g4, rep=rep, chunk=chunk, nchunks=lc // chunk),
        grid=(nseq, ng, nl),
        in_specs=[loc(wqk), loc(wqk), loc(wv), loc(wv), loc(wv), loc(wv), loc(wv),
                  loc(g4 * DEC_LANES), loc(LANE), glob(wv, zc0),
                  pl.BlockSpec((1, LANE), lambda s, h, l: (0, 0)), state,
                  pl.BlockSpec(memory_space=pl.ANY)],
        out_specs=[glob(wv, 0), state],
        out_shape=[jax.ShapeDtypeStruct(o_all.shape, o_all.dtype), jax.ShapeDtypeStruct(s0.shape, F32)],
        scratch_shapes=[pltpu.VMEM((g4, LANE, LANE), F32)],
        input_output_aliases={12: 0},
        compiler_params=_cp(("parallel", "parallel", "arbitrary")),
        name="gdn_core",
    )(q, k, kb, vb, kbg, qd, kd, dec, scal, proj, norm_w.reshape(1, LANE), s0, o_all)
    return o_new, s_out


def _sb_sub(qb, ks, vs, mfull, r, scale, row0, col0):
    z = _dot_nt(qb, ks) * scale
    ls = jnp.minimum(z, 0.0) - jnp.log(1.0 + jnp.exp(-jnp.abs(z)))
    lr = ls - z
    if row0 is not None:
        keep = (lax.broadcasted_iota(I32, z.shape, 1) + col0) < (lax.broadcasted_iota(I32, z.shape, 0) + row0)
        lr = jnp.where(keep, lr, 0.0)
    hi = lr.astype(BF16)
    lo = (lr - hi.astype(F32)).astype(BF16)
    cs = _dotf(jnp.concatenate([hi, lo], axis=1), mfull)
    wts = jnp.exp(ls + cs[:, :SB_SUB] + r)
    if row0 is not None:
        wts = jnp.where(keep, wts, 0.0)
    return _dotf(wts.astype(BF16), vs), r + cs[:, SB_SUB:]


def _sb_matrix():
    j = jnp.arange(2 * SB_SUB)[:, None] % SB_SUB
    s = jnp.arange(2 * SB_SUB)[None, :]
    return jnp.logical_or(s >= SB_SUB, j > s).astype(BF16)


def _sb_prompt_kernel(qi_ref, kj_ref, q_ref, k_ref, v_ref, m_ref, o_ref, acc, rsum, *, tile, scale):
    p = pl.program_id(2)
    qi = qi_ref[p]
    kj = kj_ref[p]

    @pl.when(kj == qi)
    def _():
        acc[...] = jnp.zeros_like(acc)
        rsum[...] = jnp.zeros_like(rsum)

    def sweep(diagonal):
        qb = q_ref[...].astype(BF16)
        mfull = m_ref[...]
        for sb in reversed(range(tile // SB_SUB)):
            rs = slice(sb * SB_SUB, (sb + 1) * SB_SUB)
            y, r = _sb_sub(qb, k_ref[rs, :].astype(BF16), v_ref[rs, :].astype(BF16), mfull, rsum[...], scale,
                           0 if diagonal else None, sb * SB_SUB)
            acc[...] += y
            rsum[...] = r

    @pl.when(kj == qi)
    def _():
        sweep(True)

    @pl.when(kj != qi)
    def _():
        sweep(False)

    @pl.when(kj == 0)
    def _():
        o_ref[...] = acc[...].astype(o_ref.dtype)


def sb_prompt(qkv, nseq, seqlen, heads, out_rows):
    tile = _row_tile(seqlen, SB_TILE, SB_SUB)
    nq = seqlen // tile
    pairs = [(qi, kj) for qi in range(nq) for kj in range(qi, -1, -1)]
    qi_tab = jnp.asarray([a for a, _ in pairs], I32)
    kj_tab = jnp.asarray([b for _, b in pairs], I32)
    grid_spec = pltpu.PrefetchScalarGridSpec(
        num_scalar_prefetch=2,
        grid=(nseq, heads, len(pairs)),
        in_specs=[pl.BlockSpec((tile, LANE), lambda b, h, p, qt, kt: (b * nq + qt[p], h)),
                  pl.BlockSpec((tile, LANE), lambda b, h, p, qt, kt: (b * nq + kt[p], heads + h)),
                  pl.BlockSpec((tile, LANE), lambda b, h, p, qt, kt: (b * nq + kt[p], 2 * heads + h)),
                  pl.BlockSpec((2 * SB_SUB, 2 * SB_SUB), lambda b, h, p, qt, kt: (0, 0))],
        out_specs=pl.BlockSpec((tile, LANE), lambda b, h, p, qt, kt: (b * nq + qt[p], h)),
        scratch_shapes=[pltpu.VMEM((tile, LANE), F32), pltpu.VMEM((tile, LANE), F32)],
    )
    return pl.pallas_call(
        functools.partial(_sb_prompt_kernel, tile=tile, scale=float(LANE) ** -0.5),
        grid_spec=grid_spec,
        out_shape=jax.ShapeDtypeStruct((out_rows, heads * LANE), BF16),
        compiler_params=_cp(("parallel", "parallel", "arbitrary")),
        name="sb_prompt",
    )(qi_tab, kj_tab, qkv, qkv, qkv, _sb_matrix())


def _sb_decode_kernel(q_ref, kn_ref, vn_ref, kc_ref, vc_ref, m_ref, oprev_ref, o_ref, acc, rsum, *, heads, ls, tk, scale):
    del oprev_ref
    j = pl.program_id(1)
    mfull = m_ref[...]

    @pl.when(j == 0)
    def _():
        pad = jnp.zeros((SB_SUB - ls, LANE), BF16)
        for h in range(heads):
            sl = slice(h * LANE, (h + 1) * LANE)
            kn = jnp.concatenate([kn_ref[:, sl].astype(BF16), pad], axis=0)
            vn = jnp.concatenate([vn_ref[:, sl].astype(BF16), pad], axis=0)
            y, r = _sb_sub(q_ref[:, sl].astype(BF16), kn, vn, mfull, jnp.zeros((ls, SB_SUB), F32), scale, 0, 0)
            acc[:, sl] = y
            rsum[h] = r

    @pl.when(j > 0)
    def _():
        for h in range(heads):
            sl = slice(h * LANE, (h + 1) * LANE)
            qb = q_ref[:, sl].astype(BF16)
            for sb in reversed(range(tk // SB_SUB)):
                rs = slice(sb * SB_SUB, (sb + 1) * SB_SUB)
                y, r = _sb_sub(qb, kc_ref[rs, sl].astype(BF16), vc_ref[rs, sl].astype(BF16), mfull, rsum[h], scale, None, None)
                acc[:, sl] += y
                rsum[h] = r

    @pl.when(j == pl.num_programs(1) - 1)
    def _():
        o_ref[...] = acc[...].astype(o_ref.dtype)


def sb_decode(qkv, row0, k_past, v_past, heads, o_all):
    nb, past, width = k_past.shape
    ls = (qkv.shape[0] - row0) // nb
    assert ls <= SB_SUB and ls % SUBLANE == 0
    tk = _row_tile(past, 512, SB_SUB)
    nkb = past // tk
    rb0 = row0 // ls

    def cache_idx(b, j):
        return (b, jnp.where(j == 0, nkb - 1, nkb - j), 0)

    return pl.pallas_call(
        functools.partial(_sb_decode_kernel, heads=heads, ls=ls, tk=tk, scale=float(LANE) ** -0.5),
        grid=(nb, nkb + 1),
        in_specs=[pl.BlockSpec((ls, width), lambda b, j: (rb0 + b, 0)),
                  pl.BlockSpec((ls, width), lambda b, j: (rb0 + b, 1)),
                  pl.BlockSpec((ls, width), lambda b, j: (rb0 + b, 2)),
                  pl.BlockSpec((None, tk, width), cache_idx),
                  pl.BlockSpec((None, tk, width), cache_idx),
                  pl.BlockSpec((2 * SB_SUB, 2 * SB_SUB), lambda b, j: (0, 0)),
                  pl.BlockSpec(memory_space=pl.ANY)],
        out_specs=pl.BlockSpec((ls, width), lambda b, j: (rb0 + b, 0)),
        out_shape=jax.ShapeDtypeStruct(o_all.shape, o_all.dtype),
        scratch_shapes=[pltpu.VMEM((ls, width), F32), pltpu.VMEM((heads, ls, SB_SUB), F32)],
        input_output_aliases={6: 0},
        compiler_params=_cp(("parallel", "arbitrary")),
        name="sb_decode",
    )(qkv, qkv, qkv, k_past, v_past, _sb_matrix(), o_all)


def _gdn_layer(x, tp, nseq_p, nseq_s, s_state, conv_state, norm_mix_w, w_in, conv_w, a_log, dt_bias, norm_w, w_out):
    t, d = x.shape
    _, hv, dk, dv = s_state.shape
    assert dk == LANE and dv == LANE
    qkv_dim = conv_state.shape[-1]
    v_dim = hv * dv
    qk_dim = (qkv_dim - v_dim) // 2
    g4 = GDN_GROUP
    ng = hv // g4
    lp = tp // nseq_p
    lsm = (t - tp) // nseq_s
    assert lp % CHUNK == 0 and lsm <= CHUNK and lsm >= CONV_W - 1

    h = rmsnorm(x, norm_mix_w, BF16)
    wide = qkv_dim + v_dim
    proj = matmul(h, w_in[:, :wide].astype(BF16))
    wb = w_in[:, wide:wide + hv].reshape(d, ng, g4)
    wa = w_in[:, wide + hv:wide + 2 * hv].reshape(d, ng, g4)
    w_ba = jnp.concatenate([wb, wa, jnp.zeros((d, ng, LANE - 2 * g4), F32)], axis=2).reshape(d, ng * LANE)
    ba = matmul(h, w_ba.astype(BF16))

    def lanes(vec):
        zero = jnp.zeros((ng, g4), F32)
        return jnp.concatenate([zero, vec.reshape(ng, g4), jnp.zeros((ng, LANE - 2 * g4), F32)], axis=1).reshape(1, ng * LANE)

    alog2, dtb2 = lanes(a_log), lanes(dt_bias)
    o_all = jnp.zeros((t, v_dim), BF16)
    states, convs = [], []
    parts = ((0, nseq_p, lp, CHUNK, jnp.zeros((nseq_p,) + s_state.shape[1:], F32), jnp.zeros((nseq_p, SUBLANE, qkv_dim), F32)),
             (tp, nseq_s, lsm, lsm, s_state,
              jnp.concatenate([jnp.zeros((nseq_s, SUBLANE - (CONV_W - 1), qkv_dim), F32), conv_state], axis=1)))
    for row0, nseq, seqlen, chunk, s0, hist in parts:
        rows = nseq * seqlen
        scal, dec = gdn_gates(lax.slice_in_dim(ba, row0, row0 + rows, axis=0), alog2, dtb2, chunk)
        pre = gdn_conv(proj, row0, nseq, seqlen, hist, conv_w, scal, qk_dim, v_dim, dk)
        o_all, s_new = gdn_core(pre, dec, scal, proj, row0, nseq, seqlen, chunk, s0, norm_w, qkv_dim, o_all)
        states.append(s_new)
        raw = lax.slice(proj, (row0, 0), (row0 + rows, qkv_dim)).reshape(nseq, seqlen, qkv_dim)
        convs.append(raw[:, seqlen - (CONV_W - 1):, :])
    x = matmul(o_all, w_out.astype(BF16), resid=x)
    return x, states, convs


def _sb_layer(x, tp, nseq_p, nseq_s, k_past, v_past, norm_mix_w, w_qkv, w_out):
    t, d = x.shape
    nb, past, heads, hd = k_past.shape
    assert hd == LANE and nb == nseq_s
    width = heads * hd
    h = rmsnorm(x, norm_mix_w, BF16)
    qkv = matmul(h, w_qkv.astype(BF16))
    o = sb_prompt(qkv, nseq_p, tp // nseq_p, heads, t)
    o = sb_decode(qkv, tp, k_past.reshape(nb, past, width), v_past.reshape(nb, past, width), heads, o)
    x = matmul(o, w_out.astype(BF16), resid=x)
    lp, lsm = tp // nseq_p, (t - tp) // nseq_s
    k_new = qkv[:, width:2 * width]
    v_new = qkv[:, 2 * width:]
    ks = (k_new[:tp].reshape(nseq_p, lp, heads, hd), k_new[tp:].reshape(nseq_s, lsm, heads, hd))
    vs = (v_new[:tp].reshape(nseq_p, lp, heads, hd), v_new[tp:].reshape(nseq_s, lsm, heads, hd))
    return x, ks, vs


def _moe_layer(x, norm_w, router, wg, wu, wd):
    t, d = x.shape
    n_experts = router.shape[1]
    h, route = norm_route(x, norm_w, router)
    expert = route[:, :TOP_K].astype(I32).reshape(-1)
    gate = route[:, TOP_K:2 * TOP_K]
    nslots = t * TOP_K
    onehot = (expert[:, None] == jnp.arange(n_experts, dtype=I32)[None, :]).astype(I32)
    counts = jnp.sum(onehot, axis=0)
    padded = (counts + MOE_TILE - 1) // MOE_TILE * MOE_TILE
    ends = jnp.cumsum(padded)
    starts = ends - padded
    rank = jnp.sum((jnp.cumsum(onehot, axis=0) - onehot) * onehot, axis=1)
    pos = starts[expert] + rank
    ntiles = (nslots + n_experts * (MOE_TILE - 1) + MOE_TILE - 1) // MOE_TILE
    tile_expert = jnp.minimum(jnp.searchsorted(ends, jnp.arange(ntiles, dtype=I32) * MOE_TILE, side="right"), n_experts - 1).astype(I32)
    n_used = (ends[-1] // MOE_TILE).astype(I32).reshape(1)
    row_token = jnp.zeros((ntiles * MOE_TILE,), I32).at[pos].set(jnp.arange(nslots, dtype=I32) // TOP_K)
    xs = jnp.take(h, row_token, axis=0)
    ys = moe_ffn(xs, tile_expert, n_used, wg.astype(BF16), wu.astype(BF16), wd.astype(BF16))
    yk = jnp.take(ys, pos, axis=0).reshape(t, TOP_K, d)
    return x + jnp.sum(yk * gate[:, :, None], axis=1)


def kernel(x_prompt, x_sample, state_gdn_S, state_gdn_conv, cache_sb_k, cache_sb_v, norm_mix, norm_ffn, norm_final, gdn_w_in, gdn_conv_w, gdn_a_log, gdn_dt_bias, gdn_norm_w, gdn_w_out, sb_w_qkv, sb_w_out, ffn_w_gate, ffn_w_up, ffn_w_down, moe_router, moe_w_gate, moe_w_up, moe_w_down):
    bp, lp, d = x_prompt.shape
    bs, lsm, _ = x_sample.shape
    tp = bp * lp
    x = jnp.concatenate([x_prompt.reshape(tp, d), x_sample.reshape(bs * lsm, d)], axis=0)
    depth = norm_mix.shape[0]
    s_p, s_s, c_p, c_s, k_p, k_s, v_p, v_s = [], [], [], [], [], [], [], []
    for i in range(depth):
        j = i // 2
        if i % 2 == 0:
            x, states, convs = _gdn_layer(x, tp, bp, bs, state_gdn_S[j], state_gdn_conv[j], norm_mix[i], gdn_w_in[j],
                                          gdn_conv_w[j], gdn_a_log[j], gdn_dt_bias[j], gdn_norm_w[j], gdn_w_out[j])
            s_p.append(states[0]); s_s.append(states[1]); c_p.append(convs[0]); c_s.append(convs[1])
            h = rmsnorm(x, norm_ffn[i], BF16)
            x = ffn(h, ffn_w_gate[j].astype(BF16), ffn_w_up[j].astype(BF16), ffn_w_down[j].astype(BF16), x)
        else:
            x, ks, vs = _sb_layer(x, tp, bp, bs, cache_sb_k[j], cache_sb_v[j], norm_mix[i], sb_w_qkv[j], sb_w_out[j])
            k_p.append(ks[0]); k_s.append(ks[1]); v_p.append(vs[0]); v_s.append(vs[1])
            x = _moe_layer(x, norm_ffn[i], moe_router[j], moe_w_gate[j], moe_w_up[j], moe_w_down[j])
    y = rmsnorm(x, norm_final, F32)
    return (y[:tp].reshape(bp, lp, d), y[tp:].reshape(bs, lsm, d),
            jnp.stack(s_p), jnp.stack(c_p), jnp.stack(k_p), jnp.stack(v_p),
            jnp.stack(s_s), jnp.stack(c_s), jnp.stack(k_s), jnp.stack(v_s))
```

```python
import functools

import jax
import jax.numpy as jnp
from jax import lax
from jax.experimental import pallas as pl
from jax.experimental.pallas import tpu as pltpu

F32 = jnp.float32
BF16 = jnp.bfloat16
I32 = jnp.int32

EPS = 1e-6
LANE = 128
SUBLANE = 8
CHUNK = 64
CONV_W = 4
TOP_K = 2
SB_SUB = LANE
SB_TILE = 1024
SB_HEADS = 2
GDN_GROUP = 4
GDN_REC_GROUP = 8
GDN_BATCH = 4
DEC_LANES = GDN_BATCH * CHUNK
ATT_LANES = CHUNK
MOE_TILE = 1024
VMEM_LIMIT = 52 * 1024 * 1024
MATMUL_VMEM_BUDGET = 36 * 1024 * 1024


def _cp(sem, vmem=VMEM_LIMIT):
    return pltpu.CompilerParams(dimension_semantics=sem, vmem_limit_bytes=vmem)


def _row_tile(n, cap, mult=16):
    best = None
    for d in range(mult, min(n, cap) + 1, mult):
        if n % d == 0:
            best = d
    assert best is not None, (n, cap)
    return best


def _col_tile(n, cap):
    return _row_tile(n, cap, LANE)


def _sigmoid(x):
    return 1.0 / (1.0 + jnp.exp(-x))


def _dotf(a, b):
    return jnp.dot(a, b, preferred_element_type=F32)


def _dot_nt(a, b):
    return lax.dot_general(a, b, (((1,), (1,)), ((), ())), preferred_element_type=F32)


def _dot_tn(a, b):
    return lax.dot_general(a, b, (((0,), (0,)), ((), ())), preferred_element_type=F32)


def _split3(a):
    a1 = a.astype(BF16)
    r = a - a1.astype(F32)
    a2 = r.astype(BF16)
    a3 = (r - a2.astype(F32)).astype(BF16)
    return a1, a2, a3


def _dot_exact_rhs(a, b_bf16):
    a1, a2, a3 = _split3(a)
    return _dotf(a1, b_bf16) + _dotf(a2, b_bf16) + _dotf(a3, b_bf16)


def _dot_exact_lhs(a_bf16, b):
    b1, b2, b3 = _split3(b)
    return _dotf(a_bf16, b1) + _dotf(a_bf16, b2) + _dotf(a_bf16, b3)


def _rmsnorm_kernel(x_ref, w_ref, h_ref):
    x = x_ref[...]
    ms = jnp.mean(x * x, axis=-1, keepdims=True)
    h_ref[...] = (x * lax.rsqrt(ms + EPS) * w_ref[...]).astype(h_ref.dtype)


def rmsnorm(x, w, out_dtype):
    t, d = x.shape
    tm = _row_tile(t, 688)
    return pl.pallas_call(
        _rmsnorm_kernel,
        grid=(t // tm,),
        in_specs=[pl.BlockSpec((tm, d), lambda i: (i, 0)), pl.BlockSpec((1, d), lambda i: (0, 0))],
        out_specs=pl.BlockSpec((tm, d), lambda i: (i, 0)),
        out_shape=jax.ShapeDtypeStruct((t, d), out_dtype),
        compiler_params=_cp(("parallel",)),
        name="rmsnorm",
    )(x, w.reshape(1, d))


def _norm_route_kernel(x_ref, w_ref, r_ref, h_ref, route_ref, *, n_experts):
    x = x_ref[...]
    ms = jnp.mean(x * x, axis=-1, keepdims=True)
    h = x * lax.rsqrt(ms + EPS) * w_ref[...]
    h_ref[...] = h.astype(h_ref.dtype)
    h1, h2, h3 = _split3(h)
    r1, r2, r3 = _split3(r_ref[...])
    logits = (_dotf(h1, r1) + (_dotf(h1, r2) + _dotf(h2, r1))
              + (_dotf(h2, r2) + _dotf(h1, r3) + _dotf(h3, r1)))
    lane = lax.broadcasted_iota(I32, logits.shape, 1).astype(F32)
    neg = -jnp.inf
    l1 = jnp.where(lane < n_experts, logits, neg)
    m1 = jnp.max(l1, axis=-1, keepdims=True)
    i1 = jnp.min(jnp.where(l1 == m1, lane, float(LANE)), axis=-1, keepdims=True)
    l2 = jnp.where(lane == i1, neg, l1)
    m2 = jnp.max(l2, axis=-1, keepdims=True)
    i2 = jnp.min(jnp.where(l2 == m2, lane, float(LANE)), axis=-1, keepdims=True)
    e2 = jnp.exp(m2 - m1)
    w1 = 1.0 / (1.0 + e2)
    w2 = e2 / (1.0 + e2)
    route_ref[...] = jnp.where(lane == 0, i1, jnp.where(lane == 1, i2, jnp.where(lane == 2, w1, jnp.where(lane == 3, w2, 0.0))))


def norm_route(x, w, router):
    t, d = x.shape
    n_experts = router.shape[1]
    tm = _row_tile(t, 688)
    r_pad = jnp.zeros((d, LANE), F32).at[:, :n_experts].set(router)
    return pl.pallas_call(
        functools.partial(_norm_route_kernel, n_experts=n_experts),
        grid=(t // tm,),
        in_specs=[pl.BlockSpec((tm, d), lambda i: (i, 0)), pl.BlockSpec((1, d), lambda i: (0, 0)),
                  pl.BlockSpec((d, LANE), lambda i: (0, 0))],
        out_specs=[pl.BlockSpec((tm, d), lambda i: (i, 0)), pl.BlockSpec((tm, LANE), lambda i: (i, 0))],
        out_shape=[jax.ShapeDtypeStruct((t, d), BF16), jax.ShapeDtypeStruct((t, LANE), F32)],
        compiler_params=_cp(("parallel",)),
        name="norm_route",
    )(x, w.reshape(1, d), r_pad)


def _mm_kernel(x_ref, w_ref, o_ref):
    o_ref[...] = _dotf(x_ref[...], w_ref[...]).astype(o_ref.dtype)


def _mm_res_kernel(x_ref, w_ref, r_ref, o_ref):
    o_ref[...] = (r_ref[...] + _dotf(x_ref[...], w_ref[...])).astype(o_ref.dtype)


def matmul(x, w, resid=None, out_dtype=F32):
    t, k = x.shape
    n = w.shape[1]
    out_bytes = 4 * (3 if resid is None else 5)

    def fits(tm, tn):
        return 4 * k * (tm + tn) + out_bytes * tm * tn <= MATMUL_VMEM_BUDGET

    tm, tn = max(((a, b) for a in (_row_tile(t, c) for c in (1376, 688, 384, 128)) for b in (_col_tile(n, c) for c in (1024, 512, 256, 128))
                  if fits(a, b)), key=lambda ab: ab[0] * ab[1])
    in_specs = [pl.BlockSpec((tm, k), lambda j, i: (i, 0)), pl.BlockSpec((k, tn), lambda j, i: (0, j))]
    args = [x, w]
    kern = _mm_kernel
    if resid is not None:
        in_specs.append(pl.BlockSpec((tm, tn), lambda j, i: (i, j)))
        args.append(resid)
        kern = _mm_res_kernel
    return pl.pallas_call(
        kern,
        grid=(n // tn, t // tm),
        in_specs=in_specs,
        out_specs=pl.BlockSpec((tm, tn), lambda j, i: (i, j)),
        out_shape=jax.ShapeDtypeStruct((t, n), out_dtype),
        compiler_params=_cp(("parallel", "parallel")),
        name="matmul",
    )(*args)


def _swiglu_step(x_ref, wg_ref, wu_ref, wd_ref):
    x = x_ref[...]
    g = _dotf(x, wg_ref[...].astype(BF16))
    u = _dotf(x, wu_ref[...].astype(BF16))
    a = (g * _sigmoid(g) * u).astype(BF16)
    return _dotf(a, wd_ref[...].astype(BF16))


def _ffn_kernel(x_ref, wg_ref, wu_ref, wd_ref, r_ref, o_ref):
    f = pl.program_id(1)
    y = _swiglu_step(x_ref, wg_ref, wu_ref, wd_ref)

    @pl.when(f == 0)
    def _():
        o_ref[...] = r_ref[...] + y

    @pl.when(f > 0)
    def _():
        o_ref[...] += y


def ffn(h, wg, wu, wd, resid):
    t, d = h.shape
    dff = wg.shape[1]
    tm = _row_tile(t, 688)
    tf = _col_tile(dff, 512)
    return pl.pallas_call(
        _ffn_kernel,
        grid=(t // tm, dff // tf),
        in_specs=[pl.BlockSpec((tm, d), lambda i, f: (i, 0)),
                  pl.BlockSpec((d, tf), lambda i, f: (0, f)),
                  pl.BlockSpec((d, tf), lambda i, f: (0, f)),
                  pl.BlockSpec((tf, d), lambda i, f: (f, 0)),
                  pl.BlockSpec((tm, d), lambda i, f: (i, 0))],
        out_specs=pl.BlockSpec((tm, d), lambda i, f: (i, 0)),
        out_shape=jax.ShapeDtypeStruct((t, d), F32),
        compiler_params=_cp(("parallel", "arbitrary")),
        name="ffn",
    )(h, wg, wu, wd, resid)


def _moe_ffn_kernel(te_ref, nu_ref, x_ref, wg_ref, wu_ref, wd_ref, o_ref):
    i = pl.program_id(0)
    f = pl.program_id(1)

    @pl.when(i < nu_ref[0])
    def _():
        y = _swiglu_step(x_ref, wg_ref, wu_ref, wd_ref)

        @pl.when(f == 0)
        def _():
            o_ref[...] = y

        @pl.when(f > 0)
        def _():
            o_ref[...] += y


def moe_ffn(xs, tile_expert, n_used, wg, wu, wd):
    p, d = xs.shape
    dff = wg.shape[2]
    tm = MOE_TILE
    tf = _col_tile(dff, 256)
    nf = dff // tf

    def fidx(i, f, nu):
        return jnp.where(i < nu[0], f, nf - 1)

    grid_spec = pltpu.PrefetchScalarGridSpec(
        num_scalar_prefetch=2,
        grid=(p // tm, nf),
        in_specs=[pl.BlockSpec((tm, d), lambda i, f, te, nu: (jnp.minimum(i, nu[0] - 1), 0)),
                  pl.BlockSpec((None, d, tf), lambda i, f, te, nu: (te[i], 0, fidx(i, f, nu))),
                  pl.BlockSpec((None, d, tf), lambda i, f, te, nu: (te[i], 0, fidx(i, f, nu))),
                  pl.BlockSpec((None, tf, d), lambda i, f, te, nu: (te[i], fidx(i, f, nu), 0))],
        out_specs=pl.BlockSpec((tm, d), lambda i, f, te, nu: (i, 0)),
    )
    return pl.pallas_call(
        _moe_ffn_kernel,
        grid_spec=grid_spec,
        out_shape=jax.ShapeDtypeStruct((p, d), F32),
        compiler_params=_cp(("arbitrary", "arbitrary")),
        name="moe_ffn",
    )(tile_expert, n_used, xs, wg, wu, wd)


def _gdn_gate_kernel(ba_ref, alog_ref, dtb_ref, ltri_ref, exp_ref, scal_ref, dec_ref, *, group, chunk):
    g4 = group
    ba = ba_ref[...]
    tm, width = ba.shape
    lane = lax.broadcasted_iota(I32, ba.shape, 1) % LANE
    is_b = lane < g4
    is_a = jnp.logical_and(lane >= g4, lane < 2 * g4)
    beta = _sigmoid(ba)
    xa = ba + dtb_ref[...]
    softplus = jnp.maximum(xa, 0.0) + jnp.log1p(jnp.exp(-jnp.abs(xa)))
    g = jnp.where(is_a, -jnp.exp(alog_ref[...]) * softplus, 0.0)
    gc = _dot_exact_lhs(ltri_ref[...], g)
    gc3 = gc.reshape(tm // chunk, chunk, width)
    gl = jnp.broadcast_to(gc3[:, chunk - 1:chunk, :], gc3.shape).reshape(tm, width)
    scal = (jnp.where(is_b, beta, 0.0) + jnp.where(is_a, gc, 0.0)
            + pltpu.roll(jnp.where(is_a, gl, 0.0), g4, 1)
            + pltpu.roll(jnp.where(is_a, gl - gc, 0.0), 2 * g4, 1))
    scal_ref[...] = scal

    dw = g4 * DEC_LANES
    row = lax.broadcasted_iota(I32, (tm, dw), 0)
    lane_h = lax.broadcasted_iota(I32, (tm, dw), 1) % DEC_LANES
    i_idx = row % chunk
    j_idx = lane_h % chunk
    same = ((row // chunk) % GDN_BATCH) == (lane_h // chunk)
    on_diag = jnp.logical_and(same, i_idx == j_idx)
    lower = jnp.logical_and(same, i_idx >= j_idx)
    for nb in range(width // LANE):
        gce = _dot_exact_rhs(gc[:, nb * LANE:(nb + 1) * LANE], exp_ref[...])
        diag = jnp.where(on_diag, gce, 0.0).reshape(tm // chunk, chunk, dw)
        gcj = jnp.broadcast_to(jnp.sum(diag, axis=1, keepdims=True), diag.shape).reshape(tm, dw)
        dec = jnp.where(lower, jnp.exp(gce - gcj), 0.0)
        dec_ref[:, nb * dw:(nb + 1) * dw] = dec.astype(dec_ref.dtype)


def gdn_gates(ba, alog2, dtb2, chunk):
    rows, width = ba.shape
    tm = _row_tile(rows, 512, chunk)
    ng = width // LANE
    g4 = GDN_GROUP
    r = jnp.arange(tm)
    ltri = jnp.logical_and(r[:, None] >= r[None, :], (r[:, None] // chunk) == (r[None, :] // chunk)).astype(BF16)
    src = jnp.arange(LANE)[:, None]
    dst = jnp.arange(g4 * DEC_LANES)[None, :]
    expand = (src == g4 + dst // DEC_LANES).astype(BF16)
    return pl.pallas_call(
        functools.partial(_gdn_gate_kernel, group=g4, chunk=chunk),
        grid=(rows // tm,),
        in_specs=[pl.BlockSpec((tm, width), lambda i: (i, 0)),
                  pl.BlockSpec((1, width), lambda i: (0, 0)),
                  pl.BlockSpec((1, width), lambda i: (0, 0)),
                  pl.BlockSpec((tm, tm), lambda i: (0, 0)),
                  pl.BlockSpec((LANE, g4 * DEC_LANES), lambda i: (0, 0))],
        out_specs=[pl.BlockSpec((tm, width), lambda i: (i, 0)),
                   pl.BlockSpec((tm, ng * g4 * DEC_LANES), lambda i: (i, 0))],
        out_shape=[jax.ShapeDtypeStruct((rows, width), F32),
                   jax.ShapeDtypeStruct((rows, ng * g4 * DEC_LANES), BF16)],
        compiler_params=_cp(("parallel",)),
        name="gdn_gates",
    )(ba, alog2, dtb2, ltri, expand)


def _gdn_conv_kernel(q_ref, k_ref, v_ref, qh_ref, kh_ref, vh_ref, qs_ref, ks_ref, vs_ref,
                     wq_ref, wk_ref, wv_ref, scal_ref,
                     qo_ref, ko_ref, kbo_ref, vbo_ref, kbgo_ref, qdo_ref, kdo_ref,
                     xq_ref, xk_ref, xv_ref, *, group, rep, tm, q_scale):
    first = pl.program_id(1) == 0

    def conv_silu(x_ref, halo_ref, hist_ref, w_ref, xp_ref):
        xp_ref[:SUBLANE, :] = jnp.where(first, hist_ref[...], halo_ref[...])
        xp_ref[SUBLANE:, :] = x_ref[...]
        w = w_ref[...]
        base = SUBLANE - (CONV_W - 1)
        y = xp_ref[SUBLANE:SUBLANE + tm, :] * w[CONV_W - 1:CONV_W]
        for j in range(CONV_W - 1):
            y = y + xp_ref[base + j:base + j + tm, :] * w[j:j + 1]
        return y * _sigmoid(y)

    def l2norm_heads(y):
        outs = []
        for h in range(y.shape[1] // LANE):
            yh = y[:, h * LANE:(h + 1) * LANE]
            outs.append(yh * lax.rsqrt(jnp.sum(yh * yh, axis=-1, keepdims=True) + EPS))
        return outs

    qn = [qh * q_scale for qh in l2norm_heads(conv_silu(q_ref, qh_ref, qs_ref, wq_ref, xq_ref))]
    kn = l2norm_heads(conv_silu(k_ref, kh_ref, ks_ref, wk_ref, xk_ref))
    v = conv_silu(v_ref, vh_ref, vs_ref, wv_ref, xv_ref)
    scal = scal_ref[...]
    for hq in range(group // rep):
        sl = slice(hq * LANE, (hq + 1) * LANE)
        qo_ref[:, sl] = qn[hq].astype(BF16)
        ko_ref[:, sl] = kn[hq].astype(BF16)
    for gh in range(group):
        hq = gh // rep
        sl = slice(gh * LANE, (gh + 1) * LANE)
        beta = scal[:, gh:gh + 1]
        eg = jnp.exp(scal[:, group + gh:group + gh + 1])
        ek = jnp.exp(scal[:, 3 * group + gh:3 * group + gh + 1])
        kb = kn[hq] * beta
        kbo_ref[:, sl] = kb.astype(BF16)
        vbo_ref[:, sl] = (v[:, sl] * beta).astype(BF16)
        kbgo_ref[:, sl] = (kb * eg).astype(BF16)
        qdo_ref[:, sl] = (qn[hq] * eg).astype(BF16)
        kdo_ref[:, sl] = (kn[hq] * ek).astype(BF16)


def gdn_conv(proj, row0, nseq, seqlen, hist, conv_w, scal, qk_dim, v_dim, dk):
    g4 = GDN_GROUP
    hv = v_dim // LANE
    rep = hv // (qk_dim // LANE)
    ng = hv // g4
    wqk = (g4 // rep) * LANE
    wv = g4 * LANE
    tm = _row_tile(seqlen, 256)
    nl = seqlen // tm
    rows = nseq * seqlen
    rb0 = row0 // tm
    nqk = qk_dim // wqk
    nv0 = 2 * qk_dim // wv
    cw = jnp.zeros((SUBLANE, conv_w.shape[1]), F32).at[:CONV_W].set(conv_w)

    def main(w, c0):
        return pl.BlockSpec((tm, w), lambda s, i, j: (rb0 + s * nl + i, c0 + j))

    def halo(w, c0):
        return pl.BlockSpec((SUBLANE, w), lambda s, i, j: (jnp.maximum((row0 + s * seqlen + i * tm) // SUBLANE - 1, 0), c0 + j))

    def hst(w, c0):
        return pl.BlockSpec((None, SUBLANE, w), lambda s, i, j: (s, 0, c0 + j))

    def wts(w, c0):
        return pl.BlockSpec((SUBLANE, w), lambda s, i, j: (0, c0 + j))

    def out(w):
        return pl.BlockSpec((tm, w), lambda s, i, j: (s * nl + i, j))

    return pl.pallas_call(
        functools.partial(_gdn_conv_kernel, group=g4, rep=rep, tm=tm, q_scale=float(dk) ** -0.5),
        grid=(nseq, nl, ng),
        in_specs=[main(wqk, 0), main(wqk, nqk), main(wv, nv0),
                  halo(wqk, 0), halo(wqk, nqk), halo(wv, nv0),
                  hst(wqk, 0), hst(wqk, nqk), hst(wv, nv0),
                  wts(wqk, 0), wts(wqk, nqk), wts(wv, nv0),
                  pl.BlockSpec((tm, LANE), lambda s, i, j: (s * nl + i, j))],
        out_specs=[out(wqk), out(wqk), out(wv), out(wv), out(wv), out(wv), out(wv)],
        out_shape=[jax.ShapeDtypeStruct((rows, qk_dim), BF16)] * 2 + [jax.ShapeDtypeStruct((rows, v_dim), BF16)] * 5,
        scratch_shapes=[pltpu.VMEM((SUBLANE + tm, wqk), F32), pltpu.VMEM((SUBLANE + tm, wqk), F32),
                        pltpu.VMEM((SUBLANE + tm, wv), F32)],
        compiler_params=_cp(("parallel", "parallel", "parallel")),
        name="gdn_conv",
    )(proj, proj, proj, proj, proj, proj, hist, hist, hist, cw, cw, cw, scal)


def _gdn_prep_kernel(q_ref, k_ref, kb_ref, vb_ref, kbg_ref, dec_ref, u_ref, w_ref, attn_ref, *, group, rep, chunk):
    nrow = q_ref.shape[0]
    ii = lax.broadcasted_iota(I32, (nrow, nrow), 0)
    jj = lax.broadcasted_iota(I32, (nrow, nrow), 1)
    strict = ii > jj
    eye = (ii == jj).astype(F32)
    if chunk < ATT_LANES:
        attn_ref[...] = jnp.zeros_like(attn_ref)
    for gh in range(group):
        slq = slice((gh // rep) * LANE, (gh // rep + 1) * LANE)
        sl = slice(gh * LANE, (gh + 1) * LANE)
        dm = dec_ref[:, gh * DEC_LANES:gh * DEC_LANES + nrow].astype(F32)
        kq = _dot_nt(jnp.concatenate([kb_ref[:, sl], q_ref[:, slq]], axis=0), k_ref[:, slq])
        a = jnp.where(strict, kq[:nrow] * dm, 0.0)
        attn = (kq[nrow:] * dm).astype(BF16)
        for c in range(nrow // chunk):
            blk = slice(c * chunk, (c + 1) * chunk)
            attn_ref[blk, gh * ATT_LANES:gh * ATT_LANES + chunk] = attn[blk, blk]
        s = eye - a
        ab = a.astype(BF16)
        b = _dotf(ab, ab).astype(BF16)
        span = 2
        while span < chunk:
            if 2 * span < chunk:
                sb = _dotf(jnp.concatenate([s.astype(BF16), b], axis=0), b)
                s = s + sb[:nrow]
                b = sb[nrow:].astype(BF16)
            else:
                s = s + _dotf(s.astype(BF16), b)
            span *= 2
        sol = _dotf(s.astype(BF16), jnp.concatenate([vb_ref[:, sl], kbg_ref[:, sl]], axis=1))
        u_ref[:, sl] = sol[:, :LANE]
        w_ref[:, sl] = sol[:, LANE:].astype(BF16)


def gdn_prep(pre, dec, chunk):
    q, k, kb, vb, kbg, _, _ = pre
    g4 = GDN_GROUP
    rows, v_dim = kb.shape
    hv = v_dim // LANE
    rep = hv // (q.shape[1] // LANE)
    wqk = (g4 // rep) * LANE
    wv = g4 * LANE
    tm = GDN_BATCH * chunk
    assert rows % tm == 0

    def blk(w):
        return pl.BlockSpec((tm, w), lambda i, h: (i, h))

    return pl.pallas_call(
        functools.partial(_gdn_prep_kernel, group=g4, rep=rep, chunk=chunk),
        grid=(rows // tm, hv // g4),
        in_specs=[blk(wqk), blk(wqk), blk(wv), blk(wv), blk(wv), blk(g4 * DEC_LANES)],
        out_specs=[blk(wv), blk(wv), blk(g4 * ATT_LANES)],
        out_shape=[jax.ShapeDtypeStruct((rows, v_dim), F32), jax.ShapeDtypeStruct((rows, v_dim), BF16),
                   jax.ShapeDtypeStruct((rows, hv * ATT_LANES), BF16)],
        compiler_params=_cp(("parallel", "parallel")),
        name="gdn_prep",
    )(q, k, kb, vb, kbg, dec)


def _gdn_rec_kernel(u_ref, w_ref, qd_ref, kd_ref, attn_ref, scal_ref, z_ref, nw_ref, s0_ref, oprev_ref,
                    o_ref, sout_ref, s_scr, *, group, lane_group, chunk, nchunks):
    del oprev_ref
    l = pl.program_id(2)

    @pl.when(l == 0)
    def _():
        s_scr[...] = s0_ref[...]

    nw = nw_ref[...]

    def one_chunk(c, carry):
        r0 = pl.multiple_of(c * chunk, chunk)
        rows = pl.ds(r0, chunk)
        sc = scal_ref[pl.ds(r0, SUBLANE), :]
        for gh in range(group):
            sl = slice(gh * LANE, (gh + 1) * LANE)
            s = s_scr[gh]
            sb = s.astype(BF16)
            ws = _dotf(jnp.concatenate([w_ref[rows, sl], qd_ref[rows, sl]], axis=0), sb)
            vnb = (u_ref[rows, sl] - ws[:chunk]).astype(BF16)
            o = ws[chunk:] + _dotf(attn_ref[rows, gh * ATT_LANES:gh * ATT_LANES + chunk], vnb)
            gl_lane = (gh // lane_group) * LANE + 2 * lane_group + gh % lane_group
            s_scr[gh] = s * jnp.exp(sc[0:1, gl_lane:gl_lane + 1]) + _dot_tn(kd_ref[rows, sl], vnb)
            ms = jnp.mean(o * o, axis=-1, keepdims=True)
            zz = z_ref[rows, sl]
            o_ref[rows, sl] = (o * lax.rsqrt(ms + EPS) * nw * (zz * _sigmoid(zz))).astype(o_ref.dtype)
        return carry

    lax.fori_loop(0, nchunks, one_chunk, 0)

    @pl.when(l == pl.num_programs(2) - 1)
    def _():
        sout_ref[...] = s_scr[...]


def gdn_core(pre, dec, scal, proj, row0, nseq, seqlen, chunk, s0, norm_w, z_col0, o_all):
    u, w, attn = gdn_prep(pre, dec, chunk)
    qd, kd = pre[5], pre[6]
    g8 = GDN_REC_GROUP
    hv = qd.shape[1] // LANE
    wv = g8 * LANE
    lc = _row_tile(seqlen, 512, chunk)
    nl = seqlen // lc
    zc0 = z_col0 // wv
    rb0 = row0 // lc

    def loc(w_):
        return pl.BlockSpec((lc, w_), lambda s, h, l: (s * nl + l, h))

    def glob(w_, c0):
        return pl.BlockSpec((lc, w_), lambda s, h, l: (rb0 + s * nl + l, c0 + h))

    state = pl.BlockSpec((None, g8, LANE, LANE), lambda s, h, l: (s, h, 0, 0))
    o_new, s_out = pl.pallas_call(
        functools.partial(_gdn_rec_kernel, group=g8, lane_group=GDN_GROUP, chunk=chunk, nchunks=lc // chunk),
        grid=(nseq, hv // g8, nl),
        in_specs=[loc(wv), loc(wv), loc(wv), loc(wv), loc(g8 * ATT_LANES), loc(g8 // GDN_GROUP * LANE), glob(wv, zc0),
                  pl.BlockSpec((1, LANE), lambda s, h, l: (0, 0)), state, pl.BlockSpec(memory_space=pl.ANY)],
        out_specs=[glob(wv, 0), state],
        out_shape=[jax.ShapeDtypeStruct(o_all.shape, o_all.dtype), jax.ShapeDtypeStruct(s0.shape, F32)],
        scratch_shapes=[pltpu.VMEM((g8, LANE, LANE), F32)],
        input_output_aliases={9: 0},
        compiler_params=_cp(("parallel", "parallel", "arbitrary")),
        name="gdn_rec",
    )(u, w, qd, kd, attn, scal, proj, norm_w.reshape(1, LANE), s0, o_all)
    return o_new, s_out


def _sb_block(qb, kb, vb, mfull, r, scale, diagonal):
    nq = qb.shape[0]
    nsb = kb.shape[0] // SB_SUB
    z = _dot_nt(qb, kb) * scale
    ls = jnp.minimum(z, 0.0) - jnp.log(1.0 + jnp.exp(-jnp.abs(z)))
    lr = ls - z
    if diagonal:
        keep = lax.broadcasted_iota(I32, z.shape, 1) < lax.broadcasted_iota(I32, z.shape, 0)
        lr = jnp.where(keep, lr, 0.0)
    hi = lr.astype(BF16)
    lo = (lr - hi.astype(F32)).astype(BF16)
    stacked = jnp.concatenate(
        [jnp.concatenate([hi[:, sb * SB_SUB:(sb + 1) * SB_SUB], lo[:, sb * SB_SUB:(sb + 1) * SB_SUB]], axis=1)
         for sb in range(nsb)], axis=0)
    cs = _dotf(stacked, mfull)
    after = [None] * nsb
    for sb in reversed(range(nsb)):
        part = cs[sb * nq:(sb + 1) * nq]
        after[sb] = part[:, :SB_SUB] + r
        r = r + part[:, SB_SUB:]
    wts = jnp.exp(ls + jnp.concatenate(after, axis=1))
    if diagonal:
        wts = jnp.where(keep, wts, 0.0)
    return _dotf(wts.astype(BF16), vb), r


def _sb_matrix():
    j = jnp.arange(2 * SB_SUB)[:, None] % SB_SUB
    s = jnp.arange(2 * SB_SUB)[None, :]
    return jnp.logical_or(s >= SB_SUB, j > s).astype(BF16)


def _sb_prompt_kernel(qi_ref, kj_ref, q_ref, k_ref, v_ref, m_ref, o_ref, acc, rsum, *, tile, scale, nh):
    p = pl.program_id(2)
    qi = qi_ref[p]
    kj = kj_ref[p]

    @pl.when(kj == qi)
    def _():
        acc[...] = jnp.zeros_like(acc)
        rsum[...] = jnp.zeros_like(rsum)

    def sweep(diagonal):
        mfull = m_ref[...]
        for h in range(nh):
            sl = slice(h * LANE, (h + 1) * LANE)
            y, r = _sb_block(q_ref[:, sl].astype(BF16), k_ref[:, sl].astype(BF16), v_ref[:, sl].astype(BF16),
                             mfull, rsum[h], scale, diagonal)
            acc[:, sl] += y
            rsum[h] = r

    @pl.when(kj == qi)
    def _():
        sweep(True)

    @pl.when(kj != qi)
    def _():
        sweep(False)

    @pl.when(kj == 0)
    def _():
        o_ref[...] = acc[...].astype(o_ref.dtype)


def sb_prompt(qkv, nseq, seqlen, heads, out_rows):
    tile = _row_tile(seqlen, SB_TILE, SB_SUB)
    nq = seqlen // tile
    nh = SB_HEADS
    ng = heads // nh
    wide = nh * LANE
    pairs = [(qi, kj) for qi in range(nq) for kj in range(qi, -1, -1)]
    qi_tab = jnp.asarray([a for a, _ in pairs], I32)
    kj_tab = jnp.asarray([b for _, b in pairs], I32)
    grid_spec = pltpu.PrefetchScalarGridSpec(
        num_scalar_prefetch=2,
        grid=(nseq, ng, len(pairs)),
        in_specs=[pl.BlockSpec((tile, wide), lambda b, h, p, qt, kt: (b * nq + qt[p], h)),
                  pl.BlockSpec((tile, wide), lambda b, h, p, qt, kt: (b * nq + kt[p], ng + h)),
                  pl.BlockSpec((tile, wide), lambda b, h, p, qt, kt: (b * nq + kt[p], 2 * ng + h)),
                  pl.BlockSpec((2 * SB_SUB, 2 * SB_SUB), lambda b, h, p, qt, kt: (0, 0))],
        out_specs=pl.BlockSpec((tile, wide), lambda b, h, p, qt, kt: (b * nq + qt[p], h)),
        scratch_shapes=[pltpu.VMEM((tile, wide), F32), pltpu.VMEM((nh, tile, LANE), F32)],
    )
    return pl.pallas_call(
        functools.partial(_sb_prompt_kernel, tile=tile, scale=float(LANE) ** -0.5, nh=nh),
        grid_spec=grid_spec,
        out_shape=jax.ShapeDtypeStruct((out_rows, heads * LANE), BF16),
        compiler_params=_cp(("parallel", "parallel", "arbitrary")),
        name="sb_prompt",
    )(qi_tab, kj_tab, qkv, qkv, qkv, _sb_matrix())


def _sb_decode_kernel(q_ref, kn_ref, vn_ref, kc_ref, vc_ref, m_ref, oprev_ref, o_ref, acc, rsum, *, heads, ls, tk, scale):
    del oprev_ref
    j = pl.program_id(1)
    mfull = m_ref[...]

    @pl.when(j == 0)
    def _():
        pad = jnp.zeros((SB_SUB - ls, LANE), BF16)
        for h in range(heads):
            sl = slice(h * LANE, (h + 1) * LANE)
            kn = jnp.concatenate([kn_ref[:, sl].astype(BF16), pad], axis=0)
            vn = jnp.concatenate([vn_ref[:, sl].astype(BF16), pad], axis=0)
            y, r = _sb_block(q_ref[:, sl].astype(BF16), kn, vn, mfull, jnp.zeros((ls, SB_SUB), F32), scale, True)
            acc[:, sl] = y
            rsum[h] = r

    @pl.when(j > 0)
    def _():
        for h in range(heads):
            sl = slice(h * LANE, (h + 1) * LANE)
            y, r = _sb_block(q_ref[:, sl].astype(BF16), kc_ref[:, sl].astype(BF16), vc_ref[:, sl].astype(BF16),
                             mfull, rsum[h], scale, False)
            acc[:, sl] += y
            rsum[h] = r

    @pl.when(j == pl.num_programs(1) - 1)
    def _():
        o_ref[...] = acc[...].astype(o_ref.dtype)


def sb_decode(qkv, row0, k_past, v_past, heads, o_all):
    nb, past, width = k_past.shape
    ls = (qkv.shape[0] - row0) // nb
    assert ls <= SB_SUB and ls % SUBLANE == 0
    tk = _row_tile(past, 512, SB_SUB)
    nkb = past // tk
    rb0 = row0 // ls

    def cache_idx(b, j):
        return (b, jnp.where(j == 0, nkb - 1, nkb - j), 0)

    return pl.pallas_call(
        functools.partial(_sb_decode_kernel, heads=heads, ls=ls, tk=tk, scale=float(LANE) ** -0.5),
        grid=(nb, nkb + 1),
        in_specs=[pl.BlockSpec((ls, width), lambda b, j: (rb0 + b, 0)),
                  pl.BlockSpec((ls, width), lambda b, j: (rb0 + b, 1)),
                  pl.BlockSpec((ls, width), lambda b, j: (rb0 + b, 2)),
                  pl.BlockSpec((None, tk, width), cache_idx),
                  pl.BlockSpec((None, tk, width), cache_idx),
                  pl.BlockSpec((2 * SB_SUB, 2 * SB_SUB), lambda b, j: (0, 0)),
                  pl.BlockSpec(memory_space=pl.ANY)],
        out_specs=pl.BlockSpec((ls, width), lambda b, j: (rb0 + b, 0)),
        out_shape=jax.ShapeDtypeStruct(o_all.shape, o_all.dtype),
        scratch_shapes=[pltpu.VMEM((ls, width), F32), pltpu.VMEM((heads, ls, SB_SUB), F32)],
        input_output_aliases={6: 0},
        compiler_params=_cp(("parallel", "arbitrary")),
        name="sb_decode",
    )(qkv, qkv, qkv, k_past, v_past, _sb_matrix(), o_all)


def _gdn_layer(x, tp, nseq_p, nseq_s, s_state, conv_state, norm_mix_w, w_in, conv_w, a_log, dt_bias, norm_w, w_out):
    t, d = x.shape
    _, hv, dk, dv = s_state.shape
    assert dk == LANE and dv == LANE
    qkv_dim = conv_state.shape[-1]
    v_dim = hv * dv
    qk_dim = (qkv_dim - v_dim) // 2
    g4 = GDN_GROUP
    ng = hv // g4
    lp = tp // nseq_p
    lsm = (t - tp) // nseq_s
    assert lp % CHUNK == 0 and lsm <= CHUNK and lsm >= CONV_W - 1

    h = rmsnorm(x, norm_mix_w, BF16)
    wide = qkv_dim + v_dim
    proj = matmul(h, w_in[:, :wide].astype(BF16))
    wb = w_in[:, wide:wide + hv].reshape(d, ng, g4)
    wa = w_in[:, wide + hv:wide + 2 * hv].reshape(d, ng, g4)
    w_ba = jnp.concatenate([wb, wa, jnp.zeros((d, ng, LANE - 2 * g4), F32)], axis=2).reshape(d, ng * LANE)
    ba = matmul(h, w_ba.astype(BF16))

    def lanes(vec):
        zero = jnp.zeros((ng, g4), F32)
        return jnp.concatenate([zero, vec.reshape(ng, g4), jnp.zeros((ng, LANE - 2 * g4), F32)], axis=1).reshape(1, ng * LANE)

    alog2, dtb2 = lanes(a_log), lanes(dt_bias)
    o_all = jnp.zeros((t, v_dim), BF16)
    states, convs = [], []
    parts = ((0, nseq_p, lp, CHUNK, jnp.zeros((nseq_p,) + s_state.shape[1:], F32), jnp.zeros((nseq_p, SUBLANE, qkv_dim), F32)),
             (tp, nseq_s, lsm, lsm, s_state,
              jnp.concatenate([jnp.zeros((nseq_s, SUBLANE - (CONV_W - 1), qkv_dim), F32), conv_state], axis=1)))
    for row0, nseq, seqlen, chunk, s0, hist in parts:
        rows = nseq * seqlen
        scal, dec = gdn_gates(lax.slice_in_dim(ba, row0, row0 + rows, axis=0), alog2, dtb2, chunk)
        pre = gdn_conv(proj, row0, nseq, seqlen, hist, conv_w, scal, qk_dim, v_dim, dk)
        o_all, s_new = gdn_core(pre, dec, scal, proj, row0, nseq, seqlen, chunk, s0, norm_w, qkv_dim, o_all)
        states.append(s_new)
        tails = [lax.slice(proj, (row0 + (s + 1) * seqlen - (CONV_W - 1), 0), (row0 + (s + 1) * seqlen, qkv_dim))
                 for s in range(nseq)]
        convs.append(jnp.stack(tails))
    x = matmul(o_all, w_out.astype(BF16), resid=x)
    return x, states, convs


def _sb_layer(x, tp, nseq_p, nseq_s, k_past, v_past, norm_mix_w, w_qkv, w_out):
    t, d = x.shape
    nb, past, heads, hd = k_past.shape
    assert hd == LANE and nb == nseq_s
    width = heads * hd
    h = rmsnorm(x, norm_mix_w, BF16)
    qkv = matmul(h, w_qkv.astype(BF16))
    o = sb_prompt(qkv, nseq_p, tp // nseq_p, heads, t)
    o = sb_decode(qkv, tp, k_past.reshape(nb, past, width), v_past.reshape(nb, past, width), heads, o)
    x = matmul(o, w_out.astype(BF16), resid=x)
    lp, lsm = tp // nseq_p, (t - tp) // nseq_s
    k_new = qkv[:, width:2 * width]
    v_new = qkv[:, 2 * width:]
    ks = (k_new[:tp].reshape(nseq_p, lp, heads, hd), k_new[tp:].reshape(nseq_s, lsm, heads, hd))
    vs = (v_new[:tp].reshape(nseq_p, lp, heads, hd), v_new[tp:].reshape(nseq_s, lsm, heads, hd))
    return x, ks, vs


def _moe_layer(x, norm_w, router, wg, wu, wd):
    t, d = x.shape
    n_experts = router.shape[1]
    h, route = norm_route(x, norm_w, router)
    expert = route[:, :TOP_K].astype(I32).reshape(-1)
    gate = route[:, TOP_K:2 * TOP_K]
    nslots = t * TOP_K
    onehot = (expert[:, None] == jnp.arange(n_experts, dtype=I32)[None, :]).astype(I32)
    counts = jnp.sum(onehot, axis=0)
    padded = (counts + MOE_TILE - 1) // MOE_TILE * MOE_TILE
    ends = jnp.cumsum(padded)
    starts = ends - padded
    rank = jnp.sum((jnp.cumsum(onehot, axis=0) - onehot) * onehot, axis=1)
    pos = starts[expert] + rank
    ntiles = (nslots + n_experts * (MOE_TILE - 1) + MOE_TILE - 1) // MOE_TILE
    tile_expert = jnp.minimum(jnp.searchsorted(ends, jnp.arange(ntiles, dtype=I32) * MOE_TILE, side="right"), n_experts - 1).astype(I32)
    n_used = (ends[-1] // MOE_TILE).astype(I32).reshape(1)
    row_token = jnp.zeros((ntiles * MOE_TILE,), I32).at[pos].set(jnp.arange(nslots, dtype=I32) // TOP_K)
    xs = jnp.take(h, row_token, axis=0)
    ys = moe_ffn(xs, tile_expert, n_used, wg, wu, wd)
    yk = jnp.take(ys, pos, axis=0).reshape(t, TOP_K, d)
    return x + jnp.sum(yk * gate[:, :, None], axis=1)


def kernel(x_prompt, x_sample, state_gdn_S, state_gdn_conv, cache_sb_k, cache_sb_v, norm_mix, norm_ffn, norm_final, gdn_w_in, gdn_conv_w, gdn_a_log, gdn_dt_bias, gdn_norm_w, gdn_w_out, sb_w_qkv, sb_w_out, ffn_w_gate, ffn_w_up, ffn_w_down, moe_router, moe_w_gate, moe_w_up, moe_w_down):
    bp, lp, d = x_prompt.shape
    bs, lsm, _ = x_sample.shape
    tp = bp * lp
    x = jnp.concatenate([x_prompt.reshape(tp, d), x_sample.reshape(bs * lsm, d)], axis=0)
    depth = norm_mix.shape[0]
    s_p, s_s, c_p, c_s, k_p, k_s, v_p, v_s = [], [], [], [], [], [], [], []
    for i in range(depth):
        j = i // 2
        if i % 2 == 0:
            x, states, convs = _gdn_layer(x, tp, bp, bs, state_gdn_S[j], state_gdn_conv[j], norm_mix[i], gdn_w_in[j],
                                          gdn_conv_w[j], gdn_a_log[j], gdn_dt_bias[j], gdn_norm_w[j], gdn_w_out[j])
            s_p.append(states[0]); s_s.append(states[1]); c_p.append(convs[0]); c_s.append(convs[1])
            h = rmsnorm(x, norm_ffn[i], BF16)
            x = ffn(h, ffn_w_gate[j].astype(BF16), ffn_w_up[j].astype(BF16), ffn_w_down[j].astype(BF16), x)
        else:
            x, ks, vs = _sb_layer(x, tp, bp, bs, cache_sb_k[j], cache_sb_v[j], norm_mix[i], sb_w_qkv[j], sb_w_out[j])
            k_p.append(ks[0]); k_s.append(ks[1]); v_p.append(vs[0]); v_s.append(vs[1])
            x = _moe_layer(x, norm_ffn[i], moe_router[j], moe_w_gate[j], moe_w_up[j], moe_w_down[j])
    y = rmsnorm(x, norm_final, F32)
    return (y[:tp].reshape(bp, lp, d), y[tp:].reshape(bs, lsm, d),
            jnp.stack(s_p), jnp.stack(c_p), jnp.stack(k_p), jnp.stack(v_p),
            jnp.stack(s_s), jnp.stack(c_s), jnp.stack(k_s), jnp.stack(v_s))
```

```python
import functools

import jax
import jax.numpy as jnp
from jax import lax
from jax.experimental import pallas as pl
from jax.experimental.pallas import tpu as pltpu

F32 = jnp.float32
BF16 = jnp.bfloat16
I32 = jnp.int32

EPS = 1e-6
LANE = 128
SUBLANE = 8
CHUNK = 64
CONV_W = 4
TOP_K = 2
SB_SUB = LANE
SB_TILE = 1024
SB_HEADS = 2
GDN_GROUP = 4
GDN_REC_GROUP = 8
GDN_BATCH = 4
DEC_LANES = GDN_BATCH * CHUNK
ATT_LANES = CHUNK
MOE_TILE = 1024
VMEM_LIMIT = 52 * 1024 * 1024
MATMUL_VMEM_BUDGET = 36 * 1024 * 1024


def _cp(sem, vmem=VMEM_LIMIT):
    return pltpu.CompilerParams(dimension_semantics=sem, vmem_limit_bytes=vmem)


def _row_tile(n, cap, mult=16):
    best = None
    for d in range(mult, min(n, cap) + 1, mult):
        if n % d == 0:
            best = d
    assert best is not None, (n, cap)
    return best


def _col_tile(n, cap):
    return _row_tile(n, cap, LANE)


def _sigmoid(x):
    return 1.0 / (1.0 + jnp.exp(-x))


def _dotf(a, b):
    return jnp.dot(a, b, preferred_element_type=F32)


def _dot_nt(a, b):
    return lax.dot_general(a, b, (((1,), (1,)), ((), ())), preferred_element_type=F32)


def _dot_tn(a, b):
    return lax.dot_general(a, b, (((0,), (0,)), ((), ())), preferred_element_type=F32)


def _split3(a):
    a1 = a.astype(BF16)
    r = a - a1.astype(F32)
    a2 = r.astype(BF16)
    a3 = (r - a2.astype(F32)).astype(BF16)
    return a1, a2, a3


def _dot_exact_rhs(a, b_bf16):
    a1, a2, a3 = _split3(a)
    return _dotf(a1, b_bf16) + _dotf(a2, b_bf16) + _dotf(a3, b_bf16)


def _dot_exact_lhs(a_bf16, b):
    b1, b2, b3 = _split3(b)
    return _dotf(a_bf16, b1) + _dotf(a_bf16, b2) + _dotf(a_bf16, b3)


def _rmsnorm_kernel(x_ref, w_ref, h_ref):
    x = x_ref[...]
    ms = jnp.mean(x * x, axis=-1, keepdims=True)
    h_ref[...] = (x * lax.rsqrt(ms + EPS) * w_ref[...]).astype(h_ref.dtype)


def rmsnorm(x, w, out_dtype):
    t, d = x.shape
    tm = _row_tile(t, 688)
    return pl.pallas_call(
        _rmsnorm_kernel,
        grid=(t // tm,),
        in_specs=[pl.BlockSpec((tm, d), lambda i: (i, 0)), pl.BlockSpec((1, d), lambda i: (0, 0))],
        out_specs=pl.BlockSpec((tm, d), lambda i: (i, 0)),
        out_shape=jax.ShapeDtypeStruct((t, d), out_dtype),
        compiler_params=_cp(("parallel",)),
        name="rmsnorm",
    )(x, w.reshape(1, d))


def _norm_route_kernel(x_ref, w_ref, r_ref, h_ref, route_ref, *, n_experts):
    x = x_ref[...]
    ms = jnp.mean(x * x, axis=-1, keepdims=True)
    h = x * lax.rsqrt(ms + EPS) * w_ref[...]
    h_ref[...] = h.astype(h_ref.dtype)
    h1, h2, h3 = _split3(h)
    r1, r2, r3 = _split3(r_ref[...])
    logits = (_dotf(h1, r1) + (_dotf(h1, r2) + _dotf(h2, r1))
              + (_dotf(h2, r2) + _dotf(h1, r3) + _dotf(h3, r1)))
    lane = lax.broadcasted_iota(I32, logits.shape, 1).astype(F32)
    neg = -jnp.inf
    l1 = jnp.where(lane < n_experts, logits, neg)
    m1 = jnp.max(l1, axis=-1, keepdims=True)
    i1 = jnp.min(jnp.where(l1 == m1, lane, float(LANE)), axis=-1, keepdims=True)
    l2 = jnp.where(lane == i1, neg, l1)
    m2 = jnp.max(l2, axis=-1, keepdims=True)
    i2 = jnp.min(jnp.where(l2 == m2, lane, float(LANE)), axis=-1, keepdims=True)
    e2 = jnp.exp(m2 - m1)
    w1 = 1.0 / (1.0 + e2)
    w2 = e2 / (1.0 + e2)
    route_ref[...] = jnp.where(lane == 0, i1, jnp.where(lane == 1, i2, jnp.where(lane == 2, w1, jnp.where(lane == 3, w2, 0.0))))


def norm_route(x, w, router):
    t, d = x.shape
    n_experts = router.shape[1]
    tm = _row_tile(t, 688)
    r_pad = jnp.zeros((d, LANE), F32).at[:, :n_experts].set(router)
    return pl.pallas_call(
        functools.partial(_norm_route_kernel, n_experts=n_experts),
        grid=(t // tm,),
        in_specs=[pl.BlockSpec((tm, d), lambda i: (i, 0)), pl.BlockSpec((1, d), lambda i: (0, 0)),
                  pl.BlockSpec((d, LANE), lambda i: (0, 0))],
        out_specs=[pl.BlockSpec((tm, d), lambda i: (i, 0)), pl.BlockSpec((tm, LANE), lambda i: (i, 0))],
        out_shape=[jax.ShapeDtypeStruct((t, d), BF16), jax.ShapeDtypeStruct((t, LANE), F32)],
        compiler_params=_cp(("parallel",)),
        name="norm_route",
    )(x, w.reshape(1, d), r_pad)


def _mm_kernel(x_ref, w_ref, o_ref):
    o_ref[...] = _dotf(x_ref[...], w_ref[...]).astype(o_ref.dtype)


def _mm_res_kernel(x_ref, w_ref, r_ref, o_ref):
    o_ref[...] = (r_ref[...] + _dotf(x_ref[...], w_ref[...])).astype(o_ref.dtype)


def matmul(x, w, resid=None, out_dtype=F32):
    t, k = x.shape
    n = w.shape[1]
    out_bytes = 4 * (3 if resid is None else 5)

    def fits(tm, tn):
        return 4 * k * (tm + tn) + out_bytes * tm * tn <= MATMUL_VMEM_BUDGET

    tm, tn = max(((a, b) for a in (_row_tile(t, c) for c in (1376, 688, 384, 128)) for b in (_col_tile(n, c) for c in (1024, 512, 256, 128))
                  if fits(a, b)), key=lambda ab: ab[0] * ab[1])
    in_specs = [pl.BlockSpec((tm, k), lambda j, i: (i, 0)), pl.BlockSpec((k, tn), lambda j, i: (0, j))]
    args = [x, w]
    kern = _mm_kernel
    if resid is not None:
        in_specs.append(pl.BlockSpec((tm, tn), lambda j, i: (i, j)))
        args.append(resid)
        kern = _mm_res_kernel
    return pl.pallas_call(
        kern,
        grid=(n // tn, t // tm),
        in_specs=in_specs,
        out_specs=pl.BlockSpec((tm, tn), lambda j, i: (i, j)),
        out_shape=jax.ShapeDtypeStruct((t, n), out_dtype),
        compiler_params=_cp(("parallel", "parallel")),
        name="matmul",
    )(*args)


def _mm_heads_kernel(x_ref, w_ref, o_ref, o3_ref):
    y = _dotf(x_ref[...], w_ref[...])
    o_ref[...] = y.astype(o_ref.dtype)
    for h in range(o3_ref.shape[1]):
        o3_ref[:, h, :] = y[:, h * LANE:(h + 1) * LANE]


def matmul_heads(x, w):
    t, k = x.shape
    n = w.shape[1]
    tm = _row_tile(t, 688)
    tn = _col_tile(n, SUBLANE * LANE)
    return pl.pallas_call(
        _mm_heads_kernel,
        grid=(n // tn, t // tm),
        in_specs=[pl.BlockSpec((tm, k), lambda j, i: (i, 0)), pl.BlockSpec((k, tn), lambda j, i: (0, j))],
        out_specs=[pl.BlockSpec((tm, tn), lambda j, i: (i, j)),
                   pl.BlockSpec((tm, tn // LANE, LANE), lambda j, i: (i, j, 0))],
        out_shape=[jax.ShapeDtypeStruct((t, n), BF16), jax.ShapeDtypeStruct((t, n // LANE, LANE), F32)],
        compiler_params=_cp(("parallel", "parallel")),
        name="matmul_heads",
    )(x, w)


def _swiglu_step(x_ref, wg_ref, wu_ref, wd_ref):
    x = x_ref[...]
    g = _dotf(x, wg_ref[...].astype(BF16))
    u = _dotf(x, wu_ref[...].astype(BF16))
    a = (g * _sigmoid(g) * u).astype(BF16)
    return _dotf(a, wd_ref[...].astype(BF16))


def _ffn_kernel(x_ref, wg_ref, wu_ref, wd_ref, r_ref, o_ref):
    f = pl.program_id(1)
    y = _swiglu_step(x_ref, wg_ref, wu_ref, wd_ref)

    @pl.when(f == 0)
    def _():
        o_ref[...] = r_ref[...] + y

    @pl.when(f > 0)
    def _():
        o_ref[...] += y


def ffn(h, wg, wu, wd, resid):
    t, d = h.shape
    dff = wg.shape[1]
    tm = _row_tile(t, 688)
    tf = _col_tile(dff, 512)
    return pl.pallas_call(
        _ffn_kernel,
        grid=(t // tm, dff // tf),
        in_specs=[pl.BlockSpec((tm, d), lambda i, f: (i, 0)),
                  pl.BlockSpec((d, tf), lambda i, f: (0, f)),
                  pl.BlockSpec((d, tf), lambda i, f: (0, f)),
                  pl.BlockSpec((tf, d), lambda i, f: (f, 0)),
                  pl.BlockSpec((tm, d), lambda i, f: (i, 0))],
        out_specs=pl.BlockSpec((tm, d), lambda i, f: (i, 0)),
        out_shape=jax.ShapeDtypeStruct((t, d), F32),
        compiler_params=_cp(("parallel", "arbitrary")),
        name="ffn",
    )(h, wg, wu, wd, resid)


def _moe_ffn_kernel(te_ref, nu_ref, x_ref, wg_ref, wu_ref, wd_ref, o_ref):
    i = pl.program_id(0)
    f = pl.program_id(1)

    @pl.when(i < nu_ref[0])
    def _():
        y = _swiglu_step(x_ref, wg_ref, wu_ref, wd_ref)

        @pl.when(f == 0)
        def _():
            o_ref[...] = y

        @pl.when(f > 0)
        def _():
            o_ref[...] += y


def moe_ffn(xs, tile_expert, n_used, wg, wu, wd):
    p, d = xs.shape
    dff = wg.shape[2]
    tm = MOE_TILE
    tf = _col_tile(dff, 256)
    nf = dff // tf

    def fidx(i, f, nu):
        return jnp.where(i < nu[0], f, nf - 1)

    grid_spec = pltpu.PrefetchScalarGridSpec(
        num_scalar_prefetch=2,
        grid=(p // tm, nf),
        in_specs=[pl.BlockSpec((tm, d), lambda i, f, te, nu: (jnp.minimum(i, nu[0] - 1), 0)),
                  pl.BlockSpec((None, d, tf), lambda i, f, te, nu: (te[i], 0, fidx(i, f, nu))),
                  pl.BlockSpec((None, d, tf), lambda i, f, te, nu: (te[i], 0, fidx(i, f, nu))),
                  pl.BlockSpec((None, tf, d), lambda i, f, te, nu: (te[i], fidx(i, f, nu), 0))],
        out_specs=pl.BlockSpec((tm, d), lambda i, f, te, nu: (i, 0)),
    )
    return pl.pallas_call(
        _moe_ffn_kernel,
        grid_spec=grid_spec,
        out_shape=jax.ShapeDtypeStruct((p, d), F32),
        compiler_params=_cp(("arbitrary", "arbitrary")),
        name="moe_ffn",
    )(tile_expert, n_used, xs, wg, wu, wd)


def _gdn_gate_kernel(ba_ref, alog_ref, dtb_ref, ltri_ref, exp_ref, scal_ref, dec_ref, *, group, chunk):
    g4 = group
    ba = ba_ref[...]
    tm, width = ba.shape
    lane = lax.broadcasted_iota(I32, ba.shape, 1) % LANE
    is_b = lane < g4
    is_a = jnp.logical_and(lane >= g4, lane < 2 * g4)
    beta = _sigmoid(ba)
    xa = ba + dtb_ref[...]
    softplus = jnp.maximum(xa, 0.0) + jnp.log1p(jnp.exp(-jnp.abs(xa)))
    g = jnp.where(is_a, -jnp.exp(alog_ref[...]) * softplus, 0.0)
    gc = _dot_exact_lhs(ltri_ref[...], g)
    gc3 = gc.reshape(tm // chunk, chunk, width)
    gl = jnp.broadcast_to(gc3[:, chunk - 1:chunk, :], gc3.shape).reshape(tm, width)
    scal = (jnp.where(is_b, beta, 0.0) + jnp.where(is_a, gc, 0.0)
            + pltpu.roll(jnp.where(is_a, gl, 0.0), g4, 1)
            + pltpu.roll(jnp.where(is_a, gl - gc, 0.0), 2 * g4, 1))
    scal_ref[...] = scal

    dw = g4 * DEC_LANES
    row = lax.broadcasted_iota(I32, (tm, dw), 0)
    lane_h = lax.broadcasted_iota(I32, (tm, dw), 1) % DEC_LANES
    i_idx = row % chunk
    j_idx = lane_h % chunk
    same = ((row // chunk) % GDN_BATCH) == (lane_h // chunk)
    on_diag = jnp.logical_and(same, i_idx == j_idx)
    lower = jnp.logical_and(same, i_idx >= j_idx)
    for nb in range(width // LANE):
        gce = _dot_exact_rhs(gc[:, nb * LANE:(nb + 1) * LANE], exp_ref[...])
        diag = jnp.where(on_diag, gce, 0.0).reshape(tm // chunk, chunk, dw)
        gcj = jnp.broadcast_to(jnp.sum(diag, axis=1, keepdims=True), diag.shape).reshape(tm, dw)
        dec = jnp.where(lower, jnp.exp(gce - gcj), 0.0)
        dec_ref[:, nb * dw:(nb + 1) * dw] = dec.astype(dec_ref.dtype)


def gdn_gates(ba, alog2, dtb2, chunk):
    rows, width = ba.shape
    tm = _row_tile(rows, 512, chunk)
    ng = width // LANE
    g4 = GDN_GROUP
    r = jnp.arange(tm)
    ltri = jnp.logical_and(r[:, None] >= r[None, :], (r[:, None] // chunk) == (r[None, :] // chunk)).astype(BF16)
    src = jnp.arange(LANE)[:, None]
    dst = jnp.arange(g4 * DEC_LANES)[None, :]
    expand = (src == g4 + dst // DEC_LANES).astype(BF16)
    return pl.pallas_call(
        functools.partial(_gdn_gate_kernel, group=g4, chunk=chunk),
        grid=(rows // tm,),
        in_specs=[pl.BlockSpec((tm, width), lambda i: (i, 0)),
                  pl.BlockSpec((1, width), lambda i: (0, 0)),
                  pl.BlockSpec((1, width), lambda i: (0, 0)),
                  pl.BlockSpec((tm, tm), lambda i: (0, 0)),
                  pl.BlockSpec((LANE, g4 * DEC_LANES), lambda i: (0, 0))],
        out_specs=[pl.BlockSpec((tm, width), lambda i: (i, 0)),
                   pl.BlockSpec((tm, ng * g4 * DEC_LANES), lambda i: (i, 0))],
        out_shape=[jax.ShapeDtypeStruct((rows, width), F32),
                   jax.ShapeDtypeStruct((rows, ng * g4 * DEC_LANES), BF16)],
        compiler_params=_cp(("parallel",)),
        name="gdn_gates",
    )(ba, alog2, dtb2, ltri, expand)


def _gdn_conv_kernel(q_ref, k_ref, v_ref, qh_ref, kh_ref, vh_ref, qs_ref, ks_ref, vs_ref,
                     wq_ref, wk_ref, wv_ref, scal_ref,
                     qo_ref, ko_ref, kbo_ref, vbo_ref, kbgo_ref, qdo_ref, kdo_ref,
                     xq_ref, xk_ref, xv_ref, *, group, rep, tm, q_scale):
    first = pl.program_id(1) == 0

    def conv_silu(x_ref, halo_ref, hist_ref, w_ref, xp_ref):
        xp_ref[:SUBLANE, :] = jnp.where(first, hist_ref[...], halo_ref[...])
        xp_ref[SUBLANE:, :] = x_ref[...]
        w = w_ref[...]
        base = SUBLANE - (CONV_W - 1)
        y = xp_ref[SUBLANE:SUBLANE + tm, :] * w[CONV_W - 1:CONV_W]
        for j in range(CONV_W - 1):
            y = y + xp_ref[base + j:base + j + tm, :] * w[j:j + 1]
        return y * _sigmoid(y)

    def l2norm_heads(y):
        outs = []
        for h in range(y.shape[1] // LANE):
            yh = y[:, h * LANE:(h + 1) * LANE]
            outs.append(yh * lax.rsqrt(jnp.sum(yh * yh, axis=-1, keepdims=True) + EPS))
        return outs

    qn = [qh * q_scale for qh in l2norm_heads(conv_silu(q_ref, qh_ref, qs_ref, wq_ref, xq_ref))]
    kn = l2norm_heads(conv_silu(k_ref, kh_ref, ks_ref, wk_ref, xk_ref))
    v = conv_silu(v_ref, vh_ref, vs_ref, wv_ref, xv_ref)
    scal = scal_ref[...]
    for hq in range(group // rep):
        sl = slice(hq * LANE, (hq + 1) * LANE)
        qo_ref[:, sl] = qn[hq].astype(BF16)
        ko_ref[:, sl] = kn[hq].astype(BF16)
    for gh in range(group):
        hq = gh // rep
        sl = slice(gh * LANE, (gh + 1) * LANE)
        beta = scal[:, gh:gh + 1]
        eg = jnp.exp(scal[:, group + gh:group + gh + 1])
        ek = jnp.exp(scal[:, 3 * group + gh:3 * group + gh + 1])
        kb = kn[hq] * beta
        kbo_ref[:, sl] = kb.astype(BF16)
        vbo_ref[:, sl] = (v[:, sl] * beta).astype(BF16)
        kbgo_ref[:, sl] = (kb * eg).astype(BF16)
        qdo_ref[:, sl] = (qn[hq] * eg).astype(BF16)
        kdo_ref[:, sl] = (kn[hq] * ek).astype(BF16)


def gdn_conv(proj, row0, nseq, seqlen, hist, conv_w, scal, qk_dim, v_dim, dk):
    g4 = GDN_GROUP
    hv = v_dim // LANE
    rep = hv // (qk_dim // LANE)
    ng = hv // g4
    wqk = (g4 // rep) * LANE
    wv = g4 * LANE
    tm = _row_tile(seqlen, 256)
    nl = seqlen // tm
    rows = nseq * seqlen
    rb0 = row0 // tm
    nqk = qk_dim // wqk
    nv0 = 2 * qk_dim // wv
    cw = jnp.zeros((SUBLANE, conv_w.shape[1]), F32).at[:CONV_W].set(conv_w)

    def main(w, c0):
        return pl.BlockSpec((tm, w), lambda s, i, j: (rb0 + s * nl + i, c0 + j))

    def halo(w, c0):
        return pl.BlockSpec((SUBLANE, w), lambda s, i, j: (jnp.maximum((row0 + s * seqlen + i * tm) // SUBLANE - 1, 0), c0 + j))

    def hst(w, c0):
        return pl.BlockSpec((None, SUBLANE, w), lambda s, i, j: (s, 0, c0 + j))

    def wts(w, c0):
        return pl.BlockSpec((SUBLANE, w), lambda s, i, j: (0, c0 + j))

    def out(w):
        return pl.BlockSpec((tm, w), lambda s, i, j: (s * nl + i, j))

    return pl.pallas_call(
        functools.partial(_gdn_conv_kernel, group=g4, rep=rep, tm=tm, q_scale=float(dk) ** -0.5),
        grid=(nseq, nl, ng),
        in_specs=[main(wqk, 0), main(wqk, nqk), main(wv, nv0),
                  halo(wqk, 0), halo(wqk, nqk), halo(wv, nv0),
                  hst(wqk, 0), hst(wqk, nqk), hst(wv, nv0),
                  wts(wqk, 0), wts(wqk, nqk), wts(wv, nv0),
                  pl.BlockSpec((tm, LANE), lambda s, i, j: (s * nl + i, j))],
        out_specs=[out(wqk), out(wqk), out(wv), out(wv), out(wv), out(wv), out(wv)],
        out_shape=[jax.ShapeDtypeStruct((rows, qk_dim), BF16)] * 2 + [jax.ShapeDtypeStruct((rows, v_dim), BF16)] * 5,
        scratch_shapes=[pltpu.VMEM((SUBLANE + tm, wqk), F32), pltpu.VMEM((SUBLANE + tm, wqk), F32),
                        pltpu.VMEM((SUBLANE + tm, wv), F32)],
        compiler_params=_cp(("parallel", "parallel", "parallel")),
        name="gdn_conv",
    )(proj, proj, proj, proj, proj, proj, hist, hist, hist, cw, cw, cw, scal)


def _gdn_prep_kernel(q_ref, k_ref, kb_ref, vb_ref, kbg_ref, dec_ref, u_ref, w_ref, attn_ref, *, group, rep, chunk):
    nrow = q_ref.shape[0]
    ii = lax.broadcasted_iota(I32, (nrow, nrow), 0)
    jj = lax.broadcasted_iota(I32, (nrow, nrow), 1)
    strict = ii > jj
    eye = (ii == jj).astype(F32)
    if chunk < ATT_LANES:
        attn_ref[...] = jnp.zeros_like(attn_ref)
    kq_of = {}
    for hq in range(group // rep):
        slq = slice(hq * LANE, (hq + 1) * LANE)
        lhs = [kb_ref[:, (hq * rep + r) * LANE:(hq * rep + r + 1) * LANE] for r in range(rep)] + [q_ref[:, slq]]
        kq = _dot_nt(jnp.concatenate(lhs, axis=0), k_ref[:, slq])
        for r in range(rep):
            kq_of[hq * rep + r] = (kq[r * nrow:(r + 1) * nrow], kq[rep * nrow:])
    for gh in range(group):
        sl = slice(gh * LANE, (gh + 1) * LANE)
        dm = dec_ref[:, gh * DEC_LANES:gh * DEC_LANES + nrow].astype(F32)
        kk, qk = kq_of[gh]
        a = jnp.where(strict, kk * dm, 0.0)
        attn = (qk * dm).astype(BF16)
        for c in range(nrow // chunk):
            blk = slice(c * chunk, (c + 1) * chunk)
            attn_ref[blk, gh * ATT_LANES:gh * ATT_LANES + chunk] = attn[blk, blk]
        s = eye - a
        ab = a.astype(BF16)
        b = _dotf(ab, ab).astype(BF16)
        span = 2
        while span < chunk:
            if 2 * span < chunk:
                sb = _dotf(jnp.concatenate([s.astype(BF16), b], axis=0), b)
                s = s + sb[:nrow]
                b = sb[nrow:].astype(BF16)
            else:
                s = s + _dotf(s.astype(BF16), b)
            span *= 2
        sol = _dotf(s.astype(BF16), jnp.concatenate([vb_ref[:, sl], kbg_ref[:, sl]], axis=1))
        u_ref[:, sl] = sol[:, :LANE]
        w_ref[:, sl] = sol[:, LANE:].astype(BF16)


def gdn_prep(pre, dec, chunk):
    q, k, kb, vb, kbg, _, _ = pre
    g4 = GDN_GROUP
    rows, v_dim = kb.shape
    hv = v_dim // LANE
    rep = hv // (q.shape[1] // LANE)
    wqk = (g4 // rep) * LANE
    wv = g4 * LANE
    tm = GDN_BATCH * chunk
    assert rows % tm == 0

    def blk(w):
        return pl.BlockSpec((tm, w), lambda i, h: (i, h))

    return pl.pallas_call(
        functools.partial(_gdn_prep_kernel, group=g4, rep=rep, chunk=chunk),
        grid=(rows // tm, hv // g4),
        in_specs=[blk(wqk), blk(wqk), blk(wv), blk(wv), blk(wv), blk(g4 * DEC_LANES)],
        out_specs=[blk(wv), blk(wv), blk(g4 * ATT_LANES)],
        out_shape=[jax.ShapeDtypeStruct((rows, v_dim), F32), jax.ShapeDtypeStruct((rows, v_dim), BF16),
                   jax.ShapeDtypeStruct((rows, hv * ATT_LANES), BF16)],
        compiler_params=_cp(("parallel", "parallel")),
        name="gdn_prep",
    )(q, k, kb, vb, kbg, dec)


def _gdn_rec_kernel(u_ref, w_ref, qd_ref, kd_ref, attn_ref, scal_ref, z_ref, nw_ref, s0_ref, *rest,
                    group, lane_group, chunk, nchunks):
    o_ref, sout_ref, s_scr = rest[-3:]
    l = pl.program_id(2)

    @pl.when(l == 0)
    def _():
        s_scr[...] = s0_ref[...]

    nw = nw_ref[...]

    def one_chunk(c, carry):
        r0 = pl.multiple_of(c * chunk, chunk)
        rows = pl.ds(r0, chunk)
        sc = scal_ref[pl.ds(r0, SUBLANE), :]
        for gh in range(group):
            sl = slice(gh * LANE, (gh + 1) * LANE)
            s = s_scr[gh]
            sb = s.astype(BF16)
            ws = _dotf(jnp.concatenate([w_ref[rows, sl], qd_ref[rows, sl]], axis=0), sb)
            vnb = (u_ref[rows, sl] - ws[:chunk]).astype(BF16)
            kdt = kd_ref[rows, sl].astype(F32).T.astype(BF16)
            both = _dotf(jnp.concatenate([attn_ref[rows, gh * ATT_LANES:gh * ATT_LANES + chunk], kdt], axis=0), vnb)
            o = ws[chunk:] + both[:chunk]
            gl_lane = (gh // lane_group) * LANE + 2 * lane_group + gh % lane_group
            s_scr[gh] = s * jnp.exp(sc[0:1, gl_lane:gl_lane + 1]) + both[chunk:]
            ms = jnp.mean(o * o, axis=-1, keepdims=True)
            zz = z_ref[rows, sl]
            o_ref[rows, sl] = (o * lax.rsqrt(ms + EPS) * nw * (zz * _sigmoid(zz))).astype(o_ref.dtype)
        return carry

    lax.fori_loop(0, nchunks, one_chunk, 0)

    @pl.when(l == pl.num_programs(2) - 1)
    def _():
        sout_ref[...] = s_scr[...]


def gdn_core(pre, dec, scal, proj, row0, nseq, seqlen, chunk, s0, norm_w, z_col0, o_all, total_rows):
    u, w, attn = gdn_prep(pre, dec, chunk)
    qd, kd = pre[5], pre[6]
    g8 = GDN_REC_GROUP
    hv = qd.shape[1] // LANE
    wv = g8 * LANE
    lc = _row_tile(seqlen, 512, chunk)
    nl = seqlen // lc
    zc0 = z_col0 // wv
    rb0 = row0 // lc

    def loc(w_):
        return pl.BlockSpec((lc, w_), lambda s, h, l: (s * nl + l, h))

    def glob(w_, c0):
        return pl.BlockSpec((lc, w_), lambda s, h, l: (rb0 + s * nl + l, c0 + h))

    state = pl.BlockSpec((None, g8, LANE, LANE), lambda s, h, l: (s, h, 0, 0))
    in_specs = [loc(wv), loc(wv), loc(wv), loc(wv), loc(g8 * ATT_LANES), loc(g8 // GDN_GROUP * LANE), glob(wv, zc0),
                pl.BlockSpec((1, LANE), lambda s, h, l: (0, 0)), state]
    args = [u, w, qd, kd, attn, scal, proj, norm_w.reshape(1, LANE), s0]
    aliases = {}
    if o_all is not None:
        aliases = {len(args): 0}
        in_specs.append(pl.BlockSpec(memory_space=pl.ANY))
        args.append(o_all)
    o_new, s_out = pl.pallas_call(
        functools.partial(_gdn_rec_kernel, group=g8, lane_group=GDN_GROUP, chunk=chunk, nchunks=lc // chunk),
        grid=(nseq, hv // g8, nl),
        in_specs=in_specs,
        out_specs=[glob(wv, 0), state],
        out_shape=[jax.ShapeDtypeStruct((total_rows, hv * LANE), BF16), jax.ShapeDtypeStruct(s0.shape, F32)],
        scratch_shapes=[pltpu.VMEM((g8, LANE, LANE), F32)],
        input_output_aliases=aliases,
        compiler_params=_cp(("parallel", "parallel", "arbitrary")),
        name="gdn_rec",
    )(*args)
    return o_new, s_out


def _sb_block(qb, kb, vb, mfull, r, scale, diagonal):
    nq = qb.shape[0]
    nsb = kb.shape[0] // SB_SUB
    z = _dot_nt(qb, kb) * scale
    ls = jnp.minimum(z, 0.0) - jnp.log(1.0 + jnp.exp(-jnp.abs(z)))
    lr = ls - z
    if diagonal:
        keep = lax.broadcasted_iota(I32, z.shape, 1) < lax.broadcasted_iota(I32, z.shape, 0)
        lr = jnp.where(keep, lr, 0.0)
    hi = lr.astype(BF16)
    lo = (lr - hi.astype(F32)).astype(BF16)
    stacked = jnp.concatenate(
        [jnp.concatenate([hi[:, sb * SB_SUB:(sb + 1) * SB_SUB], lo[:, sb * SB_SUB:(sb + 1) * SB_SUB]], axis=1)
         for sb in range(nsb)], axis=0)
    cs = _dotf(stacked, mfull)
    after = [None] * nsb
    for sb in reversed(range(nsb)):
        part = cs[sb * nq:(sb + 1) * nq]
        after[sb] = part[:, :SB_SUB] + r
        r = r + part[:, SB_SUB:]
    wts = jnp.exp(ls + jnp.concatenate(after, axis=1))
    if diagonal:
        wts = jnp.where(keep, wts, 0.0)
    return _dotf(wts.astype(BF16), vb), r


def _sb_matrix():
    j = jnp.arange(2 * SB_SUB)[:, None] % SB_SUB
    s = jnp.arange(2 * SB_SUB)[None, :]
    return jnp.logical_or(s >= SB_SUB, j > s).astype(BF16)


def _sb_prompt_kernel(qi_ref, kj_ref, q_ref, k_ref, v_ref, m_ref, o_ref, acc, rsum, *, tile, scale, nh):
    p = pl.program_id(2)
    qi = qi_ref[p]
    kj = kj_ref[p]

    @pl.when(kj == qi)
    def _():
        acc[...] = jnp.zeros_like(acc)
        rsum[...] = jnp.zeros_like(rsum)

    def sweep(diagonal):
        mfull = m_ref[...]
        for h in range(nh):
            sl = slice(h * LANE, (h + 1) * LANE)
            y, r = _sb_block(q_ref[:, sl].astype(BF16), k_ref[:, sl].astype(BF16), v_ref[:, sl].astype(BF16),
                             mfull, rsum[h], scale, diagonal)
            acc[:, sl] += y
            rsum[h] = r

    @pl.when(kj == qi)
    def _():
        sweep(True)

    @pl.when(kj != qi)
    def _():
        sweep(False)

    @pl.when(kj == 0)
    def _():
        o_ref[...] = acc[...].astype(o_ref.dtype)


def sb_prompt(q, k, v, nseq, seqlen, heads, out_rows):
    tile = _row_tile(seqlen, SB_TILE, SB_SUB)
    nq = seqlen // tile
    nh = SB_HEADS
    ng = heads // nh
    wide = nh * LANE
    pairs = [(qi, kj) for qi in range(nq) for kj in range(qi, -1, -1)]
    qi_tab = jnp.asarray([a for a, _ in pairs], I32)
    kj_tab = jnp.asarray([b for _, b in pairs], I32)
    grid_spec = pltpu.PrefetchScalarGridSpec(
        num_scalar_prefetch=2,
        grid=(nseq, ng, len(pairs)),
        in_specs=[pl.BlockSpec((tile, wide), lambda b, h, p, qt, kt: (b * nq + qt[p], h)),
                  pl.BlockSpec((tile, wide), lambda b, h, p, qt, kt: (b * nq + kt[p], h)),
                  pl.BlockSpec((tile, wide), lambda b, h, p, qt, kt: (b * nq + kt[p], h)),
                  pl.BlockSpec((2 * SB_SUB, 2 * SB_SUB), lambda b, h, p, qt, kt: (0, 0))],
        out_specs=pl.BlockSpec((tile, wide), lambda b, h, p, qt, kt: (b * nq + qt[p], h)),
        scratch_shapes=[pltpu.VMEM((tile, wide), F32), pltpu.VMEM((nh, tile, LANE), F32)],
    )
    return pl.pallas_call(
        functools.partial(_sb_prompt_kernel, tile=tile, scale=float(LANE) ** -0.5, nh=nh),
        grid_spec=grid_spec,
        out_shape=jax.ShapeDtypeStruct((out_rows, heads * LANE), BF16),
        compiler_params=_cp(("parallel", "parallel", "arbitrary")),
        name="sb_prompt",
    )(qi_tab, kj_tab, q, k, v, _sb_matrix())


def _sb_decode_kernel(q_ref, kn_ref, vn_ref, kc_ref, vc_ref, m_ref, oprev_ref, o_ref, acc, rsum, *, heads, ls, tk, scale):
    del oprev_ref
    j = pl.program_id(1)
    mfull = m_ref[...]

    @pl.when(j == 0)
    def _():
        pad = jnp.zeros((SB_SUB - ls, LANE), BF16)
        for h in range(heads):
            sl = slice(h * LANE, (h + 1) * LANE)
            kn = jnp.concatenate([kn_ref[:, sl].astype(BF16), pad], axis=0)
            vn = jnp.concatenate([vn_ref[:, sl].astype(BF16), pad], axis=0)
            y, r = _sb_block(q_ref[:, sl].astype(BF16), kn, vn, mfull, jnp.zeros((ls, SB_SUB), F32), scale, True)
            acc[:, sl] = y
            rsum[h] = r

    @pl.when(j > 0)
    def _():
        for h in range(heads):
            sl = slice(h * LANE, (h + 1) * LANE)
            y, r = _sb_block(q_ref[:, sl].astype(BF16), kc_ref[:, h, :].astype(BF16), vc_ref[:, h, :].astype(BF16),
                             mfull, rsum[h], scale, False)
            acc[:, sl] += y
            rsum[h] = r

    @pl.when(j == pl.num_programs(1) - 1)
    def _():
        o_ref[...] = acc[...].astype(o_ref.dtype)


def sb_decode(q, k, v, row0, cache_k, cache_v, layer, o_all):
    _, nb, past, heads, hd = cache_k.shape
    width = heads * hd
    ls = (q.shape[0] - row0) // nb
    assert ls <= SB_SUB and ls % SUBLANE == 0
    tk = _row_tile(past, 512, SB_SUB)
    nkb = past // tk
    rb0 = row0 // ls

    def cache_idx(b, j):
        return (layer, b, jnp.where(j == 0, nkb - 1, nkb - j), 0, 0)

    return pl.pallas_call(
        functools.partial(_sb_decode_kernel, heads=heads, ls=ls, tk=tk, scale=float(LANE) ** -0.5),
        grid=(nb, nkb + 1),
        in_specs=[pl.BlockSpec((ls, width), lambda b, j: (rb0 + b, 0)),
                  pl.BlockSpec((ls, width), lambda b, j: (rb0 + b, 0)),
                  pl.BlockSpec((ls, width), lambda b, j: (rb0 + b, 0)),
                  pl.BlockSpec((None, None, tk, heads, hd), cache_idx),
                  pl.BlockSpec((None, None, tk, heads, hd), cache_idx),
                  pl.BlockSpec((2 * SB_SUB, 2 * SB_SUB), lambda b, j: (0, 0)),
                  pl.BlockSpec(memory_space=pl.ANY)],
        out_specs=pl.BlockSpec((ls, width), lambda b, j: (rb0 + b, 0)),
        out_shape=jax.ShapeDtypeStruct(o_all.shape, o_all.dtype),
        scratch_shapes=[pltpu.VMEM((ls, width), F32), pltpu.VMEM((heads, ls, SB_SUB), F32)],
        input_output_aliases={6: 0},
        compiler_params=_cp(("parallel", "arbitrary")),
        name="sb_decode",
    )(q, k, v, cache_k, cache_v, _sb_matrix(), o_all)


def _gdn_layer(x, tp, nseq_p, nseq_s, s_state, conv_state, norm_mix_w, w_in, conv_w, a_log, dt_bias, norm_w, w_out):
    t, d = x.shape
    _, hv, dk, dv = s_state.shape
    assert dk == LANE and dv == LANE
    qkv_dim = conv_state.shape[-1]
    v_dim = hv * dv
    qk_dim = (qkv_dim - v_dim) // 2
    g4 = GDN_GROUP
    ng = hv // g4
    lp = tp // nseq_p
    lsm = (t - tp) // nseq_s
    assert lp % CHUNK == 0 and lsm <= CHUNK and lsm >= CONV_W - 1

    h = rmsnorm(x, norm_mix_w, BF16)
    wide = qkv_dim + v_dim
    proj = matmul(h, w_in[:, :wide].astype(BF16))
    wb = w_in[:, wide:wide + hv].reshape(d, ng, g4)
    wa = w_in[:, wide + hv:wide + 2 * hv].reshape(d, ng, g4)
    w_ba = jnp.concatenate([wb, wa, jnp.zeros((d, ng, LANE - 2 * g4), F32)], axis=2).reshape(d, ng * LANE)
    ba = matmul(h, w_ba.astype(BF16))

    def lanes(vec):
        zero = jnp.zeros((ng, g4), F32)
        return jnp.concatenate([zero, vec.reshape(ng, g4), jnp.zeros((ng, LANE - 2 * g4), F32)], axis=1).reshape(1, ng * LANE)

    alog2, dtb2 = lanes(a_log), lanes(dt_bias)
    o_all = None
    states, convs = [], []
    parts = ((0, nseq_p, lp, CHUNK, jnp.zeros((nseq_p,) + s_state.shape[1:], F32), jnp.zeros((nseq_p, SUBLANE, qkv_dim), F32)),
             (tp, nseq_s, lsm, lsm, s_state,
              jnp.concatenate([jnp.zeros((nseq_s, SUBLANE - (CONV_W - 1), qkv_dim), F32), conv_state], axis=1)))
    for row0, nseq, seqlen, chunk, s0, hist in parts:
        rows = nseq * seqlen
        scal, dec = gdn_gates(lax.slice_in_dim(ba, row0, row0 + rows, axis=0), alog2, dtb2, chunk)
        pre = gdn_conv(proj, row0, nseq, seqlen, hist, conv_w, scal, qk_dim, v_dim, dk)
        o_all, s_new = gdn_core(pre, dec, scal, proj, row0, nseq, seqlen, chunk, s0, norm_w, qkv_dim, o_all, t)
        states.append(s_new)
        tails = [lax.slice(proj, (row0 + (s + 1) * seqlen - (CONV_W - 1), 0), (row0 + (s + 1) * seqlen, qkv_dim))
                 for s in range(nseq)]
        convs.append(jnp.stack(tails))
    x = matmul(o_all, w_out.astype(BF16), resid=x)
    return x, states, convs


def _sb_layer(x, tp, nseq_p, nseq_s, cache_k, cache_v, layer, norm_mix_w, w_qkv, w_out):
    t, d = x.shape
    _, nb, past, heads, hd = cache_k.shape
    assert hd == LANE and nb == nseq_s
    width = heads * hd
    h = rmsnorm(x, norm_mix_w, BF16)
    wb = w_qkv.astype(BF16)
    q = matmul(h, wb[:, :width], out_dtype=BF16)
    k, k_new = matmul_heads(h, wb[:, width:2 * width])
    v, v_new = matmul_heads(h, wb[:, 2 * width:])
    o = sb_prompt(q, k, v, nseq_p, tp // nseq_p, heads, t)
    o = sb_decode(q, k, v, tp, cache_k, cache_v, layer, o)
    x = matmul(o, w_out.astype(BF16), resid=x)
    lp, lsm = tp // nseq_p, (t - tp) // nseq_s
    ks = (k_new[:tp].reshape(nseq_p, lp, heads, hd), k_new[tp:].reshape(nseq_s, lsm, heads, hd))
    vs = (v_new[:tp].reshape(nseq_p, lp, heads, hd), v_new[tp:].reshape(nseq_s, lsm, heads, hd))
    return x, ks, vs


def _moe_layer(x, norm_w, router, wg, wu, wd):
    t, d = x.shape
    n_experts = router.shape[1]
    h, route = norm_route(x, norm_w, router)
    expert = route[:, :TOP_K].astype(I32).reshape(-1)
    gate = route[:, TOP_K:2 * TOP_K]
    nslots = t * TOP_K
    onehot = (expert[:, None] == jnp.arange(n_experts, dtype=I32)[None, :]).astype(I32)
    counts = jnp.sum(onehot, axis=0)
    padded = (counts + MOE_TILE - 1) // MOE_TILE * MOE_TILE
    ends = jnp.cumsum(padded)
    starts = ends - padded
    rank = jnp.sum((jnp.cumsum(onehot, axis=0) - onehot) * onehot, axis=1)
    pos = starts[expert] + rank
    ntiles = (nslots + n_experts * (MOE_TILE - 1) + MOE_TILE - 1) // MOE_TILE
    tile_expert = jnp.minimum(jnp.searchsorted(ends, jnp.arange(ntiles, dtype=I32) * MOE_TILE, side="right"), n_experts - 1).astype(I32)
    n_used = (ends[-1] // MOE_TILE).astype(I32).reshape(1)
    row_token = jnp.zeros((ntiles * MOE_TILE,), I32).at[pos].set(jnp.arange(nslots, dtype=I32) // TOP_K)
    xs = jnp.take(h, row_token, axis=0)
    ys = moe_ffn(xs, tile_expert, n_used, wg, wu, wd)
    yk = jnp.take(ys, pos, axis=0).reshape(t, TOP_K, d)
    return x + jnp.sum(yk * gate[:, :, None], axis=1)


def kernel(x_prompt, x_sample, state_gdn_S, state_gdn_conv, cache_sb_k, cache_sb_v, norm_mix, norm_ffn, norm_final, gdn_w_in, gdn_conv_w, gdn_a_log, gdn_dt_bias, gdn_norm_w, gdn_w_out, sb_w_qkv, sb_w_out, ffn_w_gate, ffn_w_up, ffn_w_down, moe_router, moe_w_gate, moe_w_up, moe_w_down):
    bp, lp, d = x_prompt.shape
    bs, lsm, _ = x_sample.shape
    tp = bp * lp
    x = jnp.concatenate([x_prompt.reshape(tp, d), x_sample.reshape(bs * lsm, d)], axis=0)
    depth = norm_mix.shape[0]
    s_p, s_s, c_p, c_s, k_p, k_s, v_p, v_s = [], [], [], [], [], [], [], []
    for i in range(depth):
        j = i // 2
        if i % 2 == 0:
            x, states, convs = _gdn_layer(x, tp, bp, bs, state_gdn_S[j], state_gdn_conv[j], norm_mix[i], gdn_w_in[j],
                                          gdn_conv_w[j], gdn_a_log[j], gdn_dt_bias[j], gdn_norm_w[j], gdn_w_out[j])
            s_p.append(states[0]); s_s.append(states[1]); c_p.append(convs[0]); c_s.append(convs[1])
            h = rmsnorm(x, norm_ffn[i], BF16)
            x = ffn(h, ffn_w_gate[j].astype(BF16), ffn_w_up[j].astype(BF16), ffn_w_down[j].astype(BF16), x)
        else:
            x, ks, vs = _sb_layer(x, tp, bp, bs, cache_sb_k, cache_sb_v, j, norm_mix[i], sb_w_qkv[j], sb_w_out[j])
            k_p.append(ks[0]); k_s.append(ks[1]); v_p.append(vs[0]); v_s.append(vs[1])
            x = _moe_layer(x, norm_ffn[i], moe_router[j], moe_w_gate[j], moe_w_up[j], moe_w_down[j])
    y = rmsnorm(x, norm_final, F32)
    return (y[:tp].reshape(bp, lp, d), y[tp:].reshape(bs, lsm, d),
            jnp.stack(s_p), jnp.stack(c_p), jnp.stack(k_p), jnp.stack(v_p),
            jnp.stack(s_s), jnp.stack(c_s), jnp.stack(k_s), jnp.stack(v_s))
```

```python
import functools

import jax
import jax.numpy as jnp
from jax import lax
from jax.experimental import pallas as pl
from jax.experimental.pallas import tpu as pltpu

F32 = jnp.float32
BF16 = jnp.bfloat16
I32 = jnp.int32

EPS = 1e-6
LANE = 128
SUBLANE = 8
CHUNK = 64
CONV_W = 4
TOP_K = 2
SB_SUB = LANE
SB_TILE = 1024
SB_HEADS = 2
GDN_GROUP = 4
GDN_REC_GROUP = 8
GDN_BATCH = 4
DEC_LANES = GDN_BATCH * CHUNK
ATT_LANES = CHUNK
MOE_TILE = 1024
VMEM_LIMIT = 52 * 1024 * 1024
MATMUL_VMEM_BUDGET = 36 * 1024 * 1024


def _cp(sem, vmem=VMEM_LIMIT):
    return pltpu.CompilerParams(dimension_semantics=sem, vmem_limit_bytes=vmem)


def _row_tile(n, cap, mult=16):
    best = None
    for d in range(mult, min(n, cap) + 1, mult):
        if n % d == 0:
            best = d
    assert best is not None, (n, cap)
    return best


def _col_tile(n, cap):
    return _row_tile(n, cap, LANE)


def _sigmoid(x):
    return 1.0 / (1.0 + jnp.exp(-x))


def _dotf(a, b):
    return jnp.dot(a, b, preferred_element_type=F32)


def _dot_nt(a, b):
    return lax.dot_general(a, b, (((1,), (1,)), ((), ())), preferred_element_type=F32)


def _dot_tn(a, b):
    return lax.dot_general(a, b, (((0,), (0,)), ((), ())), preferred_element_type=F32)


def _split3(a):
    a1 = a.astype(BF16)
    r = a - a1.astype(F32)
    a2 = r.astype(BF16)
    a3 = (r - a2.astype(F32)).astype(BF16)
    return a1, a2, a3


def _dot_exact_rhs(a, b_bf16):
    a1, a2, a3 = _split3(a)
    return _dotf(a1, b_bf16) + _dotf(a2, b_bf16) + _dotf(a3, b_bf16)


def _dot_exact_lhs(a_bf16, b):
    b1, b2, b3 = _split3(b)
    return _dotf(a_bf16, b1) + _dotf(a_bf16, b2) + _dotf(a_bf16, b3)


def _rmsnorm_kernel(x_ref, w_ref, h_ref):
    x = x_ref[...]
    ms = jnp.mean(x * x, axis=-1, keepdims=True)
    h_ref[...] = (x * lax.rsqrt(ms + EPS) * w_ref[...]).astype(h_ref.dtype)


def rmsnorm(x, w, out_dtype):
    t, d = x.shape
    tm = _row_tile(t, 688)
    return pl.pallas_call(
        _rmsnorm_kernel,
        grid=(t // tm,),
        in_specs=[pl.BlockSpec((tm, d), lambda i: (i, 0)), pl.BlockSpec((1, d), lambda i: (0, 0))],
        out_specs=pl.BlockSpec((tm, d), lambda i: (i, 0)),
        out_shape=jax.ShapeDtypeStruct((t, d), out_dtype),
        compiler_params=_cp(("parallel",)),
        name="rmsnorm",
    )(x, w.reshape(1, d))


def _norm_route_kernel(x_ref, w_ref, r_ref, h_ref, route_ref, *, n_experts):
    x = x_ref[...]
    ms = jnp.mean(x * x, axis=-1, keepdims=True)
    h = x * lax.rsqrt(ms + EPS) * w_ref[...]
    h_ref[...] = h.astype(h_ref.dtype)
    h1, h2, h3 = _split3(h)
    r1, r2, r3 = _split3(r_ref[...])
    logits = (_dotf(h1, r1) + (_dotf(h1, r2) + _dotf(h2, r1))
              + (_dotf(h2, r2) + _dotf(h1, r3) + _dotf(h3, r1)))
    lane = lax.broadcasted_iota(I32, logits.shape, 1).astype(F32)
    neg = -jnp.inf
    l1 = jnp.where(lane < n_experts, logits, neg)
    m1 = jnp.max(l1, axis=-1, keepdims=True)
    i1 = jnp.min(jnp.where(l1 == m1, lane, float(LANE)), axis=-1, keepdims=True)
    l2 = jnp.where(lane == i1, neg, l1)
    m2 = jnp.max(l2, axis=-1, keepdims=True)
    i2 = jnp.min(jnp.where(l2 == m2, lane, float(LANE)), axis=-1, keepdims=True)
    e2 = jnp.exp(m2 - m1)
    w1 = 1.0 / (1.0 + e2)
    w2 = e2 / (1.0 + e2)
    route_ref[...] = jnp.where(lane == 0, i1, jnp.where(lane == 1, i2, jnp.where(lane == 2, w1, jnp.where(lane == 3, w2, 0.0))))


def norm_route(x, w, router):
    t, d = x.shape
    n_experts = router.shape[1]
    tm = _row_tile(t, 688)
    r_pad = jnp.zeros((d, LANE), F32).at[:, :n_experts].set(router)
    return pl.pallas_call(
        functools.partial(_norm_route_kernel, n_experts=n_experts),
        grid=(t // tm,),
        in_specs=[pl.BlockSpec((tm, d), lambda i: (i, 0)), pl.BlockSpec((1, d), lambda i: (0, 0)),
                  pl.BlockSpec((d, LANE), lambda i: (0, 0))],
        out_specs=[pl.BlockSpec((tm, d), lambda i: (i, 0)), pl.BlockSpec((tm, LANE), lambda i: (i, 0))],
        out_shape=[jax.ShapeDtypeStruct((t, d), BF16), jax.ShapeDtypeStruct((t, LANE), F32)],
        compiler_params=_cp(("parallel",)),
        name="norm_route",
    )(x, w.reshape(1, d), r_pad)


def _mm_kernel(x_ref, w_ref, o_ref):
    o_ref[...] = _dotf(x_ref[...], w_ref[...]).astype(o_ref.dtype)


def _mm_res_kernel(x_ref, w_ref, r_ref, o_ref):
    o_ref[...] = (r_ref[...] + _dotf(x_ref[...], w_ref[...])).astype(o_ref.dtype)


def matmul(x, w, resid=None, out_dtype=F32):
    t, k = x.shape
    n = w.shape[1]
    out_bytes = 4 * (3 if resid is None else 5)

    def fits(tm, tn):
        return 4 * k * (tm + tn) + out_bytes * tm * tn <= MATMUL_VMEM_BUDGET

    tm, tn = max(((a, b) for a in (_row_tile(t, c) for c in (1376, 688, 384, 128)) for b in (_col_tile(n, c) for c in (1024, 512, 256, 128))
                  if fits(a, b)), key=lambda ab: ab[0] * ab[1])
    in_specs = [pl.BlockSpec((tm, k), lambda j, i: (i, 0)), pl.BlockSpec((k, tn), lambda j, i: (0, j))]
    args = [x, w]
    kern = _mm_kernel
    if resid is not None:
        in_specs.append(pl.BlockSpec((tm, tn), lambda j, i: (i, j)))
        args.append(resid)
        kern = _mm_res_kernel
    return pl.pallas_call(
        kern,
        grid=(n // tn, t // tm),
        in_specs=in_specs,
        out_specs=pl.BlockSpec((tm, tn), lambda j, i: (i, j)),
        out_shape=jax.ShapeDtypeStruct((t, n), out_dtype),
        compiler_params=_cp(("parallel", "parallel")),
        name="matmul",
    )(*args)


def _mm_heads_kernel(x_ref, w_ref, o_ref, o3_ref):
    y = _dotf(x_ref[...], w_ref[...])
    o_ref[...] = y.astype(o_ref.dtype)
    for h in range(o3_ref.shape[1]):
        o3_ref[:, h, :] = y[:, h * LANE:(h + 1) * LANE]


def matmul_heads(x, w):
    t, k = x.shape
    n = w.shape[1]
    tm = _row_tile(t, 688)
    tn = _col_tile(n, SUBLANE * LANE)
    return pl.pallas_call(
        _mm_heads_kernel,
        grid=(n // tn, t // tm),
        in_specs=[pl.BlockSpec((tm, k), lambda j, i: (i, 0)), pl.BlockSpec((k, tn), lambda j, i: (0, j))],
        out_specs=[pl.BlockSpec((tm, tn), lambda j, i: (i, j)),
                   pl.BlockSpec((tm, tn // LANE, LANE), lambda j, i: (i, j, 0))],
        out_shape=[jax.ShapeDtypeStruct((t, n), BF16), jax.ShapeDtypeStruct((t, n // LANE, LANE), F32)],
        compiler_params=_cp(("parallel", "parallel")),
        name="matmul_heads",
    )(x, w)


def _swiglu_step(x_ref, wg_ref, wu_ref, wd_ref):
    x = x_ref[...]
    g = _dotf(x, wg_ref[...].astype(BF16))
    u = _dotf(x, wu_ref[...].astype(BF16))
    a = (g * _sigmoid(g) * u).astype(BF16)
    return _dotf(a, wd_ref[...].astype(BF16))


def _ffn_kernel(x_ref, wg_ref, wu_ref, wd_ref, r_ref, o_ref):
    f = pl.program_id(1)
    y = _swiglu_step(x_ref, wg_ref, wu_ref, wd_ref)

    @pl.when(f == 0)
    def _():
        o_ref[...] = r_ref[...] + y

    @pl.when(f > 0)
    def _():
        o_ref[...] += y


def ffn(h, wg, wu, wd, resid):
    t, d = h.shape
    dff = wg.shape[1]
    tm = _row_tile(t, 688)
    tf = _col_tile(dff, 512)
    return pl.pallas_call(
        _ffn_kernel,
        grid=(t // tm, dff // tf),
        in_specs=[pl.BlockSpec((tm, d), lambda i, f: (i, 0)),
                  pl.BlockSpec((d, tf), lambda i, f: (0, f)),
                  pl.BlockSpec((d, tf), lambda i, f: (0, f)),
                  pl.BlockSpec((tf, d), lambda i, f: (f, 0)),
                  pl.BlockSpec((tm, d), lambda i, f: (i, 0))],
        out_specs=pl.BlockSpec((tm, d), lambda i, f: (i, 0)),
        out_shape=jax.ShapeDtypeStruct((t, d), F32),
        compiler_params=_cp(("parallel", "arbitrary")),
        name="ffn",
    )(h, wg, wu, wd, resid)


def _moe_ffn_kernel(te_ref, nu_ref, x_ref, wg_ref, wu_ref, wd_ref, o_ref):
    i = pl.program_id(0)
    f = pl.program_id(1)

    @pl.when(i < nu_ref[0])
    def _():
        y = _swiglu_step(x_ref, wg_ref, wu_ref, wd_ref)

        @pl.when(f == 0)
        def _():
            o_ref[...] = y

        @pl.when(f > 0)
        def _():
            o_ref[...] += y


def moe_ffn(xs, tile_expert, n_used, wg, wu, wd, layer):
    p, d = xs.shape
    dff = wg.shape[3]
    tm = MOE_TILE
    tf = _col_tile(dff, 256)
    nf = dff // tf

    def fidx(i, f, nu):
        return jnp.where(i < nu[0], f, nf - 1)

    grid_spec = pltpu.PrefetchScalarGridSpec(
        num_scalar_prefetch=2,
        grid=(p // tm, nf),
        in_specs=[pl.BlockSpec((tm, d), lambda i, f, te, nu: (jnp.minimum(i, nu[0] - 1), 0)),
                  pl.BlockSpec((None, None, d, tf), lambda i, f, te, nu: (layer, te[i], 0, fidx(i, f, nu))),
                  pl.BlockSpec((None, None, d, tf), lambda i, f, te, nu: (layer, te[i], 0, fidx(i, f, nu))),
                  pl.BlockSpec((None, None, tf, d), lambda i, f, te, nu: (layer, te[i], fidx(i, f, nu), 0))],
        out_specs=pl.BlockSpec((tm, d), lambda i, f, te, nu: (i, 0)),
    )
    return pl.pallas_call(
        _moe_ffn_kernel,
        grid_spec=grid_spec,
        out_shape=jax.ShapeDtypeStruct((p, d), F32),
        compiler_params=_cp(("arbitrary", "arbitrary")),
        name="moe_ffn",
    )(tile_expert, n_used, xs, wg, wu, wd)


def _gdn_gate_kernel(ba_ref, alog_ref, dtb_ref, ltri_ref, exp_ref, scal_ref, dec_ref, *, group, chunk):
    g4 = group
    ba = ba_ref[...]
    tm, width = ba.shape
    lane = lax.broadcasted_iota(I32, ba.shape, 1) % LANE
    is_b = lane < g4
    is_a = jnp.logical_and(lane >= g4, lane < 2 * g4)
    beta = _sigmoid(ba)
    xa = ba + dtb_ref[...]
    softplus = jnp.maximum(xa, 0.0) + jnp.log1p(jnp.exp(-jnp.abs(xa)))
    g = jnp.where(is_a, -jnp.exp(alog_ref[...]) * softplus, 0.0)
    gc = _dot_exact_lhs(ltri_ref[...], g)
    gc3 = gc.reshape(tm // chunk, chunk, width)
    gl = jnp.broadcast_to(gc3[:, chunk - 1:chunk, :], gc3.shape).reshape(tm, width)
    scal = (jnp.where(is_b, beta, 0.0) + jnp.where(is_a, gc, 0.0)
            + pltpu.roll(jnp.where(is_a, gl, 0.0), g4, 1)
            + pltpu.roll(jnp.where(is_a, gl - gc, 0.0), 2 * g4, 1))
    scal_ref[...] = scal

    dw = g4 * DEC_LANES
    row = lax.broadcasted_iota(I32, (tm, dw), 0)
    lane_h = lax.broadcasted_iota(I32, (tm, dw), 1) % DEC_LANES
    i_idx = row % chunk
    j_idx = lane_h % chunk
    same = ((row // chunk) % GDN_BATCH) == (lane_h // chunk)
    on_diag = jnp.logical_and(same, i_idx == j_idx)
    lower = jnp.logical_and(same, i_idx >= j_idx)
    for nb in range(width // LANE):
        gce = _dot_exact_rhs(gc[:, nb * LANE:(nb + 1) * LANE], exp_ref[...])
        diag = jnp.where(on_diag, gce, 0.0).reshape(tm // chunk, chunk, dw)
        gcj = jnp.broadcast_to(jnp.sum(diag, axis=1, keepdims=True), diag.shape).reshape(tm, dw)
        dec = jnp.where(lower, jnp.exp(gce - gcj), 0.0)
        dec_ref[:, nb * dw:(nb + 1) * dw] = dec.astype(dec_ref.dtype)


def gdn_gates(ba, alog2, dtb2, chunk):
    rows, width = ba.shape
    tm = _row_tile(rows, 512, chunk)
    ng = width // LANE
    g4 = GDN_GROUP
    r = jnp.arange(tm)
    ltri = jnp.logical_and(r[:, None] >= r[None, :], (r[:, None] // chunk) == (r[None, :] // chunk)).astype(BF16)
    src = jnp.arange(LANE)[:, None]
    dst = jnp.arange(g4 * DEC_LANES)[None, :]
    expand = (src == g4 + dst // DEC_LANES).astype(BF16)
    return pl.pallas_call(
        functools.partial(_gdn_gate_kernel, group=g4, chunk=chunk),
        grid=(rows // tm,),
        in_specs=[pl.BlockSpec((tm, width), lambda i: (i, 0)),
                  pl.BlockSpec((1, width), lambda i: (0, 0)),
                  pl.BlockSpec((1, width), lambda i: (0, 0)),
                  pl.BlockSpec((tm, tm), lambda i: (0, 0)),
                  pl.BlockSpec((LANE, g4 * DEC_LANES), lambda i: (0, 0))],
        out_specs=[pl.BlockSpec((tm, width), lambda i: (i, 0)),
                   pl.BlockSpec((tm, ng * g4 * DEC_LANES), lambda i: (i, 0))],
        out_shape=[jax.ShapeDtypeStruct((rows, width), F32),
                   jax.ShapeDtypeStruct((rows, ng * g4 * DEC_LANES), BF16)],
        compiler_params=_cp(("parallel",)),
        name="gdn_gates",
    )(ba, alog2, dtb2, ltri, expand)


def _gdn_conv_kernel(q_ref, k_ref, v_ref, qh_ref, kh_ref, vh_ref, qs_ref, ks_ref, vs_ref,
                     wq_ref, wk_ref, wv_ref, scal_ref,
                     qo_ref, ko_ref, kbo_ref, vbo_ref, kbgo_ref, qdo_ref, kdo_ref,
                     xq_ref, xk_ref, xv_ref, *, group, rep, tm, q_scale):
    first = pl.program_id(1) == 0

    def conv_silu(x_ref, halo_ref, hist_ref, w_ref, xp_ref):
        xp_ref[:SUBLANE, :] = jnp.where(first, hist_ref[...], halo_ref[...])
        xp_ref[SUBLANE:, :] = x_ref[...]
        w = w_ref[...]
        base = SUBLANE - (CONV_W - 1)
        y = xp_ref[SUBLANE:SUBLANE + tm, :] * w[CONV_W - 1:CONV_W]
        for j in range(CONV_W - 1):
            y = y + xp_ref[base + j:base + j + tm, :] * w[j:j + 1]
        return y * _sigmoid(y)

    def l2norm_heads(y):
        outs = []
        for h in range(y.shape[1] // LANE):
            yh = y[:, h * LANE:(h + 1) * LANE]
            outs.append(yh * lax.rsqrt(jnp.sum(yh * yh, axis=-1, keepdims=True) + EPS))
        return outs

    qn = [qh * q_scale for qh in l2norm_heads(conv_silu(q_ref, qh_ref, qs_ref, wq_ref, xq_ref))]
    kn = l2norm_heads(conv_silu(k_ref, kh_ref, ks_ref, wk_ref, xk_ref))
    v = conv_silu(v_ref, vh_ref, vs_ref, wv_ref, xv_ref)
    scal = scal_ref[...]
    for hq in range(group // rep):
        sl = slice(hq * LANE, (hq + 1) * LANE)
        qo_ref[:, sl] = qn[hq].astype(BF16)
        ko_ref[:, sl] = kn[hq].astype(BF16)
    for gh in range(group):
        hq = gh // rep
        sl = slice(gh * LANE, (gh + 1) * LANE)
        beta = scal[:, gh:gh + 1]
        eg = jnp.exp(scal[:, group + gh:group + gh + 1])
        ek = jnp.exp(scal[:, 3 * group + gh:3 * group + gh + 1])
        kb = kn[hq] * beta
        kbo_ref[:, sl] = kb.astype(BF16)
        vbo_ref[:, sl] = (v[:, sl] * beta).astype(BF16)
        kbgo_ref[:, sl] = (kb * eg).astype(BF16)
        qdo_ref[:, sl] = (qn[hq] * eg).astype(BF16)
        kdo_ref[:, sl] = (kn[hq] * ek).astype(BF16)


def gdn_conv(proj, row0, nseq, seqlen, hist, conv_w, scal, qk_dim, v_dim, dk):
    g4 = GDN_GROUP
    hv = v_dim // LANE
    rep = hv // (qk_dim // LANE)
    ng = hv // g4
    wqk = (g4 // rep) * LANE
    wv = g4 * LANE
    tm = _row_tile(seqlen, 256)
    nl = seqlen // tm
    rows = nseq * seqlen
    rb0 = row0 // tm
    nqk = qk_dim // wqk
    nv0 = 2 * qk_dim // wv
    cw = jnp.zeros((SUBLANE, conv_w.shape[1]), F32).at[:CONV_W].set(conv_w)

    def main(w, c0):
        return pl.BlockSpec((tm, w), lambda s, i, j: (rb0 + s * nl + i, c0 + j))

    def halo(w, c0):
        return pl.BlockSpec((SUBLANE, w), lambda s, i, j: (jnp.maximum((row0 + s * seqlen + i * tm) // SUBLANE - 1, 0), c0 + j))

    def hst(w, c0):
        return pl.BlockSpec((None, SUBLANE, w), lambda s, i, j: (s, 0, c0 + j))

    def wts(w, c0):
        return pl.BlockSpec((SUBLANE, w), lambda s, i, j: (0, c0 + j))

    def out(w):
        return pl.BlockSpec((tm, w), lambda s, i, j: (s * nl + i, j))

    return pl.pallas_call(
        functools.partial(_gdn_conv_kernel, group=g4, rep=rep, tm=tm, q_scale=float(dk) ** -0.5),
        grid=(nseq, nl, ng),
        in_specs=[main(wqk, 0), main(wqk, nqk), main(wv, nv0),
                  halo(wqk, 0), halo(wqk, nqk), halo(wv, nv0),
                  hst(wqk, 0), hst(wqk, nqk), hst(wv, nv0),
                  wts(wqk, 0), wts(wqk, nqk), wts(wv, nv0),
                  pl.BlockSpec((tm, LANE), lambda s, i, j: (s * nl + i, j))],
        out_specs=[out(wqk), out(wqk), out(wv), out(wv), out(wv), out(wv), out(wv)],
        out_shape=[jax.ShapeDtypeStruct((rows, qk_dim), BF16)] * 2 + [jax.ShapeDtypeStruct((rows, v_dim), BF16)] * 5,
        scratch_shapes=[pltpu.VMEM((SUBLANE + tm, wqk), F32), pltpu.VMEM((SUBLANE + tm, wqk), F32),
                        pltpu.VMEM((SUBLANE + tm, wv), F32)],
        compiler_params=_cp(("parallel", "parallel", "parallel")),
        name="gdn_conv",
    )(proj, proj, proj, proj, proj, proj, hist, hist, hist, cw, cw, cw, scal)


def _gdn_prep_kernel(q_ref, k_ref, kb_ref, vb_ref, kbg_ref, dec_ref, u_ref, w_ref, attn_ref, *, group, rep, chunk):
    nrow = q_ref.shape[0]
    ii = lax.broadcasted_iota(I32, (nrow, nrow), 0)
    jj = lax.broadcasted_iota(I32, (nrow, nrow), 1)
    strict = ii > jj
    eye = (ii == jj).astype(F32)
    if chunk < ATT_LANES:
        attn_ref[...] = jnp.zeros_like(attn_ref)
    kq_of = {}
    for hq in range(group // rep):
        slq = slice(hq * LANE, (hq + 1) * LANE)
        lhs = [kb_ref[:, (hq * rep + r) * LANE:(hq * rep + r + 1) * LANE] for r in range(rep)] + [q_ref[:, slq]]
        kq = _dot_nt(jnp.concatenate(lhs, axis=0), k_ref[:, slq])
        for r in range(rep):
            kq_of[hq * rep + r] = (kq[r * nrow:(r + 1) * nrow], kq[rep * nrow:])
    for gh in range(group):
        sl = slice(gh * LANE, (gh + 1) * LANE)
        dm = dec_ref[:, gh * DEC_LANES:gh * DEC_LANES + nrow].astype(F32)
        kk, qk = kq_of[gh]
        a = jnp.where(strict, kk * dm, 0.0)
        attn = (qk * dm).astype(BF16)
        for c in range(nrow // chunk):
            blk = slice(c * chunk, (c + 1) * chunk)
            attn_ref[blk, gh * ATT_LANES:gh * ATT_LANES + chunk] = attn[blk, blk]
        s = eye - a
        ab = a.astype(BF16)
        b = _dotf(ab, ab).astype(BF16)
        span = 2
        while span < chunk:
            if 2 * span < chunk:
                sb = _dotf(jnp.concatenate([s.astype(BF16), b], axis=0), b)
                s = s + sb[:nrow]
                b = sb[nrow:].astype(BF16)
            else:
                s = s + _dotf(s.astype(BF16), b)
            span *= 2
        sol = _dotf(s.astype(BF16), jnp.concatenate([vb_ref[:, sl], kbg_ref[:, sl]], axis=1))
        u_ref[:, sl] = sol[:, :LANE]
        w_ref[:, sl] = sol[:, LANE:].astype(BF16)


def gdn_prep(pre, dec, chunk):
    q, k, kb, vb, kbg, _, _ = pre
    g4 = GDN_GROUP
    rows, v_dim = kb.shape
    hv = v_dim // LANE
    rep = hv // (q.shape[1] // LANE)
    wqk = (g4 // rep) * LANE
    wv = g4 * LANE
    tm = GDN_BATCH * chunk
    assert rows % tm == 0

    def blk(w):
        return pl.BlockSpec((tm, w), lambda i, h: (i, h))

    return pl.pallas_call(
        functools.partial(_gdn_prep_kernel, group=g4, rep=rep, chunk=chunk),
        grid=(rows // tm, hv // g4),
        in_specs=[blk(wqk), blk(wqk), blk(wv), blk(wv), blk(wv), blk(g4 * DEC_LANES)],
        out_specs=[blk(wv), blk(wv), blk(g4 * ATT_LANES)],
        out_shape=[jax.ShapeDtypeStruct((rows, v_dim), F32), jax.ShapeDtypeStruct((rows, v_dim), BF16),
                   jax.ShapeDtypeStruct((rows, hv * ATT_LANES), BF16)],
        compiler_params=_cp(("parallel", "parallel")),
        name="gdn_prep",
    )(q, k, kb, vb, kbg, dec)


def _gdn_rec_kernel(u_ref, w_ref, qd_ref, kd_ref, attn_ref, scal_ref, z_ref, nw_ref, s0_ref, *rest,
                    group, lane_group, chunk, nchunks):
    o_ref, sout_ref, s_scr = rest[-3:]
    l = pl.program_id(2)

    @pl.when(l == 0)
    def _():
        s_scr[...] = s0_ref[...]

    nw = nw_ref[...]

    def one_chunk(c, carry):
        r0 = pl.multiple_of(c * chunk, chunk)
        rows = pl.ds(r0, chunk)
        sc = scal_ref[pl.ds(r0, SUBLANE), :]
        for gh in range(group):
            sl = slice(gh * LANE, (gh + 1) * LANE)
            s = s_scr[gh]
            sb = s.astype(BF16)
            ws = _dotf(jnp.concatenate([w_ref[rows, sl], qd_ref[rows, sl]], axis=0), sb)
            vnb = (u_ref[rows, sl] - ws[:chunk]).astype(BF16)
            kdt = kd_ref[rows, sl].astype(F32).T.astype(BF16)
            both = _dotf(jnp.concatenate([attn_ref[rows, gh * ATT_LANES:gh * ATT_LANES + chunk], kdt], axis=0), vnb)
            o = ws[chunk:] + both[:chunk]
            gl_lane = (gh // lane_group) * LANE + 2 * lane_group + gh % lane_group
            s_scr[gh] = s * jnp.exp(sc[0:1, gl_lane:gl_lane + 1]) + both[chunk:]
            ms = jnp.mean(o * o, axis=-1, keepdims=True)
            zz = z_ref[rows, sl]
            o_ref[rows, sl] = (o * lax.rsqrt(ms + EPS) * nw * (zz * _sigmoid(zz))).astype(o_ref.dtype)
        return carry

    lax.fori_loop(0, nchunks, one_chunk, 0)

    @pl.when(l == pl.num_programs(2) - 1)
    def _():
        sout_ref[...] = s_scr[...]


def gdn_core(pre, dec, scal, proj, row0, nseq, seqlen, chunk, s0, norm_w, z_col0, o_all, total_rows):
    u, w, attn = gdn_prep(pre, dec, chunk)
    qd, kd = pre[5], pre[6]
    g8 = GDN_REC_GROUP
    hv = qd.shape[1] // LANE
    wv = g8 * LANE
    lc = _row_tile(seqlen, 512, chunk)
    nl = seqlen // lc
    zc0 = z_col0 // wv
    rb0 = row0 // lc

    def loc(w_):
        return pl.BlockSpec((lc, w_), lambda s, h, l: (s * nl + l, h))

    def glob(w_, c0):
        return pl.BlockSpec((lc, w_), lambda s, h, l: (rb0 + s * nl + l, c0 + h))

    state = pl.BlockSpec((None, g8, LANE, LANE), lambda s, h, l: (s, h, 0, 0))
    in_specs = [loc(wv), loc(wv), loc(wv), loc(wv), loc(g8 * ATT_LANES), loc(g8 // GDN_GROUP * LANE), glob(wv, zc0),
                pl.BlockSpec((1, LANE), lambda s, h, l: (0, 0)), state]
    args = [u, w, qd, kd, attn, scal, proj, norm_w.reshape(1, LANE), s0]
    aliases = {}
    if o_all is not None:
        aliases = {len(args): 0}
        in_specs.append(pl.BlockSpec(memory_space=pl.ANY))
        args.append(o_all)
    o_new, s_out = pl.pallas_call(
        functools.partial(_gdn_rec_kernel, group=g8, lane_group=GDN_GROUP, chunk=chunk, nchunks=lc // chunk),
        grid=(nseq, hv // g8, nl),
        in_specs=in_specs,
        out_specs=[glob(wv, 0), state],
        out_shape=[jax.ShapeDtypeStruct((total_rows, hv * LANE), BF16), jax.ShapeDtypeStruct(s0.shape, F32)],
        scratch_shapes=[pltpu.VMEM((g8, LANE, LANE), F32)],
        input_output_aliases=aliases,
        compiler_params=_cp(("parallel", "parallel", "arbitrary")),
        name="gdn_rec",
    )(*args)
    return o_new, s_out


def _sb_block(qb, kb, vb, mfull, r, scale, diagonal):
    nq = qb.shape[0]
    nsb = kb.shape[0] // SB_SUB
    z = _dot_nt(qb, kb) * scale
    ls = jnp.minimum(z, 0.0) - jnp.log(1.0 + jnp.exp(-jnp.abs(z)))
    lr = ls - z
    if diagonal:
        keep = lax.broadcasted_iota(I32, z.shape, 1) < lax.broadcasted_iota(I32, z.shape, 0)
        lr = jnp.where(keep, lr, 0.0)
    hi = lr.astype(BF16)
    lo = (lr - hi.astype(F32)).astype(BF16)
    stacked = jnp.concatenate(
        [jnp.concatenate([hi[:, sb * SB_SUB:(sb + 1) * SB_SUB], lo[:, sb * SB_SUB:(sb + 1) * SB_SUB]], axis=1)
         for sb in range(nsb)], axis=0)
    cs = _dotf(stacked, mfull)
    after = [None] * nsb
    for sb in reversed(range(nsb)):
        part = cs[sb * nq:(sb + 1) * nq]
        after[sb] = part[:, :SB_SUB] + r
        r = r + part[:, SB_SUB:]
    wts = jnp.exp(ls + jnp.concatenate(after, axis=1))
    if diagonal:
        wts = jnp.where(keep, wts, 0.0)
    return _dotf(wts.astype(BF16), vb), r


def _sb_matrix():
    j = jnp.arange(2 * SB_SUB)[:, None] % SB_SUB
    s = jnp.arange(2 * SB_SUB)[None, :]
    return jnp.logical_or(s >= SB_SUB, j > s).astype(BF16)


def _sb_prompt_kernel(qi_ref, kj_ref, q_ref, k_ref, v_ref, m_ref, o_ref, acc, rsum, *, tile, scale, nh):
    p = pl.program_id(2)
    qi = qi_ref[p]
    kj = kj_ref[p]

    @pl.when(kj == qi)
    def _():
        acc[...] = jnp.zeros_like(acc)
        rsum[...] = jnp.zeros_like(rsum)

    def sweep(diagonal):
        mfull = m_ref[...]
        for h in range(nh):
            sl = slice(h * LANE, (h + 1) * LANE)
            y, r = _sb_block(q_ref[:, sl].astype(BF16), k_ref[:, sl].astype(BF16), v_ref[:, sl].astype(BF16),
                             mfull, rsum[h], scale, diagonal)
            acc[:, sl] += y
            rsum[h] = r

    @pl.when(kj == qi)
    def _():
        sweep(True)

    @pl.when(kj != qi)
    def _():
        sweep(False)

    @pl.when(kj == 0)
    def _():
        o_ref[...] = acc[...].astype(o_ref.dtype)


def sb_prompt(q, k, v, nseq, seqlen, heads, out_rows):
    tile = _row_tile(seqlen, SB_TILE, SB_SUB)
    nq = seqlen // tile
    nh = SB_HEADS
    ng = heads // nh
    wide = nh * LANE
    pairs = [(qi, kj) for qi in range(nq) for kj in range(qi, -1, -1)]
    qi_tab = jnp.asarray([a for a, _ in pairs], I32)
    kj_tab = jnp.asarray([b for _, b in pairs], I32)
    grid_spec = pltpu.PrefetchScalarGridSpec(
        num_scalar_prefetch=2,
        grid=(nseq, ng, len(pairs)),
        in_specs=[pl.BlockSpec((tile, wide), lambda b, h, p, qt, kt: (b * nq + qt[p], h)),
                  pl.BlockSpec((tile, wide), lambda b, h, p, qt, kt: (b * nq + kt[p], h)),
                  pl.BlockSpec((tile, wide), lambda b, h, p, qt, kt: (b * nq + kt[p], h)),
                  pl.BlockSpec((2 * SB_SUB, 2 * SB_SUB), lambda b, h, p, qt, kt: (0, 0))],
        out_specs=pl.BlockSpec((tile, wide), lambda b, h, p, qt, kt: (b * nq + qt[p], h)),
        scratch_shapes=[pltpu.VMEM((tile, wide), F32), pltpu.VMEM((nh, tile, LANE), F32)],
    )
    return pl.pallas_call(
        functools.partial(_sb_prompt_kernel, tile=tile, scale=float(LANE) ** -0.5, nh=nh),
        grid_spec=grid_spec,
        out_shape=jax.ShapeDtypeStruct((out_rows, heads * LANE), BF16),
        compiler_params=_cp(("parallel", "parallel", "arbitrary")),
        name="sb_prompt",
    )(qi_tab, kj_tab, q, k, v, _sb_matrix())


def _sb_decode_kernel(q_ref, kn_ref, vn_ref, kc_ref, vc_ref, m_ref, oprev_ref, o_ref, acc, rsum, *, heads, ls, tk, scale):
    del oprev_ref
    j = pl.program_id(1)
    mfull = m_ref[...]

    @pl.when(j == 0)
    def _():
        pad = jnp.zeros((SB_SUB - ls, LANE), BF16)
        for h in range(heads):
            sl = slice(h * LANE, (h + 1) * LANE)
            kn = jnp.concatenate([kn_ref[:, sl].astype(BF16), pad], axis=0)
            vn = jnp.concatenate([vn_ref[:, sl].astype(BF16), pad], axis=0)
            y, r = _sb_block(q_ref[:, sl].astype(BF16), kn, vn, mfull, jnp.zeros((ls, SB_SUB), F32), scale, True)
            acc[:, sl] = y
            rsum[h] = r

    @pl.when(j > 0)
    def _():
        for h in range(heads):
            sl = slice(h * LANE, (h + 1) * LANE)
            y, r = _sb_block(q_ref[:, sl].astype(BF16), kc_ref[:, h, :].astype(BF16), vc_ref[:, h, :].astype(BF16),
                             mfull, rsum[h], scale, False)
            acc[:, sl] += y
            rsum[h] = r

    @pl.when(j == pl.num_programs(1) - 1)
    def _():
        o_ref[...] = acc[...].astype(o_ref.dtype)


def sb_decode(q, k, v, row0, cache_k, cache_v, layer, o_all):
    _, nb, past, heads, hd = cache_k.shape
    width = heads * hd
    ls = (q.shape[0] - row0) // nb
    assert ls <= SB_SUB and ls % SUBLANE == 0
    tk = _row_tile(past, 512, SB_SUB)
    nkb = past // tk
    rb0 = row0 // ls

    def cache_idx(b, j):
        return (layer, b, jnp.where(j == 0, nkb - 1, nkb - j), 0, 0)

    return pl.pallas_call(
        functools.partial(_sb_decode_kernel, heads=heads, ls=ls, tk=tk, scale=float(LANE) ** -0.5),
        grid=(nb, nkb + 1),
        in_specs=[pl.BlockSpec((ls, width), lambda b, j: (rb0 + b, 0)),
                  pl.BlockSpec((ls, width), lambda b, j: (rb0 + b, 0)),
                  pl.BlockSpec((ls, width), lambda b, j: (rb0 + b, 0)),
                  pl.BlockSpec((None, None, tk, heads, hd), cache_idx),
                  pl.BlockSpec((None, None, tk, heads, hd), cache_idx),
                  pl.BlockSpec((2 * SB_SUB, 2 * SB_SUB), lambda b, j: (0, 0)),
                  pl.BlockSpec(memory_space=pl.ANY)],
        out_specs=pl.BlockSpec((ls, width), lambda b, j: (rb0 + b, 0)),
        out_shape=jax.ShapeDtypeStruct(o_all.shape, o_all.dtype),
        scratch_shapes=[pltpu.VMEM((ls, width), F32), pltpu.VMEM((heads, ls, SB_SUB), F32)],
        input_output_aliases={6: 0},
        compiler_params=_cp(("parallel", "arbitrary")),
        name="sb_decode",
    )(q, k, v, cache_k, cache_v, _sb_matrix(), o_all)


def _gdn_layer(x, tp, nseq_p, nseq_s, s_state, conv_state, norm_mix_w, w_in, conv_w, a_log, dt_bias, norm_w, w_out):
    t, d = x.shape
    _, hv, dk, dv = s_state.shape
    assert dk == LANE and dv == LANE
    qkv_dim = conv_state.shape[-1]
    v_dim = hv * dv
    qk_dim = (qkv_dim - v_dim) // 2
    g4 = GDN_GROUP
    ng = hv // g4
    lp = tp // nseq_p
    lsm = (t - tp) // nseq_s
    assert lp % CHUNK == 0 and lsm <= CHUNK and lsm >= CONV_W - 1

    h = rmsnorm(x, norm_mix_w, BF16)
    wide = qkv_dim + v_dim
    proj = matmul(h, w_in[:, :wide].astype(BF16))
    wb = w_in[:, wide:wide + hv].reshape(d, ng, g4)
    wa = w_in[:, wide + hv:wide + 2 * hv].reshape(d, ng, g4)
    w_ba = jnp.concatenate([wb, wa, jnp.zeros((d, ng, LANE - 2 * g4), F32)], axis=2).reshape(d, ng * LANE)
    ba = matmul(h, w_ba.astype(BF16))

    def lanes(vec):
        zero = jnp.zeros((ng, g4), F32)
        return jnp.concatenate([zero, vec.reshape(ng, g4), jnp.zeros((ng, LANE - 2 * g4), F32)], axis=1).reshape(1, ng * LANE)

    alog2, dtb2 = lanes(a_log), lanes(dt_bias)
    o_all = None
    states, convs = [], []
    parts = ((0, nseq_p, lp, CHUNK, jnp.zeros((nseq_p,) + s_state.shape[1:], F32), jnp.zeros((nseq_p, SUBLANE, qkv_dim), F32)),
             (tp, nseq_s, lsm, lsm, s_state,
              jnp.concatenate([jnp.zeros((nseq_s, SUBLANE - (CONV_W - 1), qkv_dim), F32), conv_state], axis=1)))
    for row0, nseq, seqlen, chunk, s0, hist in parts:
        rows = nseq * seqlen
        scal, dec = gdn_gates(lax.slice_in_dim(ba, row0, row0 + rows, axis=0), alog2, dtb2, chunk)
        pre = gdn_conv(proj, row0, nseq, seqlen, hist, conv_w, scal, qk_dim, v_dim, dk)
        o_all, s_new = gdn_core(pre, dec, scal, proj, row0, nseq, seqlen, chunk, s0, norm_w, qkv_dim, o_all, t)
        states.append(s_new)
        tails = [lax.slice(proj, (row0 + (s + 1) * seqlen - (CONV_W - 1), 0), (row0 + (s + 1) * seqlen, qkv_dim))
                 for s in range(nseq)]
        convs.append(jnp.stack(tails))
    x = matmul(o_all, w_out.astype(BF16), resid=x)
    return x, states, convs


def _sb_layer(x, tp, nseq_p, nseq_s, cache_k, cache_v, layer, norm_mix_w, w_qkv, w_out):
    t, d = x.shape
    _, nb, past, heads, hd = cache_k.shape
    assert hd == LANE and nb == nseq_s
    width = heads * hd
    h = rmsnorm(x, norm_mix_w, BF16)
    wb = w_qkv.astype(BF16)
    q = matmul(h, wb[:, :width], out_dtype=BF16)
    k, k_new = matmul_heads(h, wb[:, width:2 * width])
    v, v_new = matmul_heads(h, wb[:, 2 * width:])
    o = sb_prompt(q, k, v, nseq_p, tp // nseq_p, heads, t)
    o = sb_decode(q, k, v, tp, cache_k, cache_v, layer, o)
    x = matmul(o, w_out.astype(BF16), resid=x)
    lp, lsm = tp // nseq_p, (t - tp) // nseq_s
    ks = (k_new[:tp].reshape(nseq_p, lp, heads, hd), k_new[tp:].reshape(nseq_s, lsm, heads, hd))
    vs = (v_new[:tp].reshape(nseq_p, lp, heads, hd), v_new[tp:].reshape(nseq_s, lsm, heads, hd))
    return x, ks, vs


def _moe_layer(x, norm_w, router, wg, wu, wd, layer):
    t, d = x.shape
    n_experts = router.shape[1]
    h, route = norm_route(x, norm_w, router)
    expert = route[:, :TOP_K].astype(I32).reshape(-1)
    gate = route[:, TOP_K:2 * TOP_K]
    nslots = t * TOP_K
    onehot = (expert[:, None] == jnp.arange(n_experts, dtype=I32)[None, :]).astype(I32)
    counts = jnp.sum(onehot, axis=0)
    padded = (counts + MOE_TILE - 1) // MOE_TILE * MOE_TILE
    ends = jnp.cumsum(padded)
    starts = ends - padded
    rank = jnp.sum((jnp.cumsum(onehot, axis=0) - onehot) * onehot, axis=1)
    pos = starts[expert] + rank
    ntiles = (nslots + n_experts * (MOE_TILE - 1) + MOE_TILE - 1) // MOE_TILE
    tile_expert = jnp.minimum(jnp.searchsorted(ends, jnp.arange(ntiles, dtype=I32) * MOE_TILE, side="right"), n_experts - 1).astype(I32)
    n_used = (ends[-1] // MOE_TILE).astype(I32).reshape(1)
    row_token = jnp.zeros((ntiles * MOE_TILE,), I32).at[pos].set(jnp.arange(nslots, dtype=I32) // TOP_K)
    xs = jnp.take(h, row_token, axis=0)
    ys = moe_ffn(xs, tile_expert, n_used, wg, wu, wd, layer)
    pos = pos.reshape(t, TOP_K)
    for kk in range(TOP_K):
        x = x + jnp.take(ys, pos[:, kk], axis=0) * gate[:, kk:kk + 1]
    return x


def kernel(x_prompt, x_sample, state_gdn_S, state_gdn_conv, cache_sb_k, cache_sb_v, norm_mix, norm_ffn, norm_final, gdn_w_in, gdn_conv_w, gdn_a_log, gdn_dt_bias, gdn_norm_w, gdn_w_out, sb_w_qkv, sb_w_out, ffn_w_gate, ffn_w_up, ffn_w_down, moe_router, moe_w_gate, moe_w_up, moe_w_down):
    bp, lp, d = x_prompt.shape
    bs, lsm, _ = x_sample.shape
    tp = bp * lp
    x = jnp.concatenate([x_prompt.reshape(tp, d), x_sample.reshape(bs * lsm, d)], axis=0)
    depth = norm_mix.shape[0]
    s_p, s_s, c_p, c_s, k_p, k_s, v_p, v_s = [], [], [], [], [], [], [], []
    for i in range(depth):
        j = i // 2
        if i % 2 == 0:
            x, states, convs = _gdn_layer(x, tp, bp, bs, state_gdn_S[j], state_gdn_conv[j], norm_mix[i], gdn_w_in[j],
                                          gdn_conv_w[j], gdn_a_log[j], gdn_dt_bias[j], gdn_norm_w[j], gdn_w_out[j])
            s_p.append(states[0]); s_s.append(states[1]); c_p.append(convs[0]); c_s.append(convs[1])
            h = rmsnorm(x, norm_ffn[i], BF16)
            x = ffn(h, ffn_w_gate[j].astype(BF16), ffn_w_up[j].astype(BF16), ffn_w_down[j].astype(BF16), x)
        else:
            x, ks, vs = _sb_layer(x, tp, bp, bs, cache_sb_k, cache_sb_v, j, norm_mix[i], sb_w_qkv[j], sb_w_out[j])
            k_p.append(ks[0]); k_s.append(ks[1]); v_p.append(vs[0]); v_s.append(vs[1])
            x = _moe_layer(x, norm_ffn[i], moe_router[j], moe_w_gate, moe_w_up, moe_w_down, j)
    y = rmsnorm(x, norm_final, F32)
    return (y[:tp].reshape(bp, lp, d), y[tp:].reshape(bs, lsm, d),
            jnp.stack(s_p), jnp.stack(c_p), jnp.stack(k_p), jnp.stack(v_p),
            jnp.stack(s_s), jnp.stack(c_s), jnp.stack(k_s), jnp.stack(v_s))
```

```python
import functools

import jax
import jax.numpy as jnp
from jax import lax
from jax.experimental import pallas as pl
from jax.experimental.pallas import tpu as pltpu

F32 = jnp.float32
BF16 = jnp.bfloat16
I32 = jnp.int32

EPS = 1e-6
LANE = 128
SUBLANE = 8
CHUNK = 64
CONV_W = 4
TOP_K = 2
SB_SUB = LANE
SB_TILE = 1024
SB_HEADS = 2
GDN_GROUP = 4
GDN_REC_GROUP = 8
GDN_BATCH = 4
DEC_LANES = GDN_BATCH * CHUNK
ATT_LANES = CHUNK
MOE_TILE = 1024
VMEM_LIMIT = 52 * 1024 * 1024
MATMUL_VMEM_BUDGET = 36 * 1024 * 1024


def _cp(sem, vmem=VMEM_LIMIT):
    return pltpu.CompilerParams(dimension_semantics=sem, vmem_limit_bytes=vmem)


def _row_tile(n, cap, mult=16):
    best = None
    for d in range(mult, min(n, cap) + 1, mult):
        if n % d == 0:
            best = d
    assert best is not None, (n, cap)
    return best


def _col_tile(n, cap):
    return _row_tile(n, cap, LANE)


def _sigmoid(x):
    return 1.0 / (1.0 + jnp.exp(-x))


def _dotf(a, b):
    return jnp.dot(a, b, preferred_element_type=F32)


def _dot_nt(a, b):
    return lax.dot_general(a, b, (((1,), (1,)), ((), ())), preferred_element_type=F32)


def _dot_tn(a, b):
    return lax.dot_general(a, b, (((0,), (0,)), ((), ())), preferred_element_type=F32)


def _split3(a):
    a1 = a.astype(BF16)
    r = a - a1.astype(F32)
    a2 = r.astype(BF16)
    a3 = (r - a2.astype(F32)).astype(BF16)
    return a1, a2, a3


def _dot_exact_rhs(a, b_bf16):
    a1, a2, a3 = _split3(a)
    return _dotf(a1, b_bf16) + _dotf(a2, b_bf16) + _dotf(a3, b_bf16)


def _dot_exact_lhs(a_bf16, b):
    b1, b2, b3 = _split3(b)
    return _dotf(a_bf16, b1) + _dotf(a_bf16, b2) + _dotf(a_bf16, b3)


def _rmsnorm_kernel(x_ref, w_ref, h_ref):
    x = x_ref[...]
    ms = jnp.mean(x * x, axis=-1, keepdims=True)
    h_ref[...] = (x * lax.rsqrt(ms + EPS) * w_ref[...]).astype(h_ref.dtype)


def rmsnorm(x, w, out_dtype):
    t, d = x.shape
    tm = _row_tile(t, 688)
    return pl.pallas_call(
        _rmsnorm_kernel,
        grid=(t // tm,),
        in_specs=[pl.BlockSpec((tm, d), lambda i: (i, 0)), pl.BlockSpec((1, d), lambda i: (0, 0))],
        out_specs=pl.BlockSpec((tm, d), lambda i: (i, 0)),
        out_shape=jax.ShapeDtypeStruct((t, d), out_dtype),
        compiler_params=_cp(("parallel",)),
        name="rmsnorm",
    )(x, w.reshape(1, d))


def _norm_route_kernel(x_ref, w_ref, r_ref, h_ref, route_ref, *, n_experts):
    x = x_ref[...]
    ms = jnp.mean(x * x, axis=-1, keepdims=True)
    h = x * lax.rsqrt(ms + EPS) * w_ref[...]
    h_ref[...] = h.astype(h_ref.dtype)
    h1, h2, h3 = _split3(h)
    r1, r2, r3 = _split3(r_ref[...])
    logits = (_dotf(h1, r1) + (_dotf(h1, r2) + _dotf(h2, r1))
              + (_dotf(h2, r2) + _dotf(h1, r3) + _dotf(h3, r1)))
    lane = lax.broadcasted_iota(I32, logits.shape, 1).astype(F32)
    neg = -jnp.inf
    l1 = jnp.where(lane < n_experts, logits, neg)
    m1 = jnp.max(l1, axis=-1, keepdims=True)
    i1 = jnp.min(jnp.where(l1 == m1, lane, float(LANE)), axis=-1, keepdims=True)
    l2 = jnp.where(lane == i1, neg, l1)
    m2 = jnp.max(l2, axis=-1, keepdims=True)
    i2 = jnp.min(jnp.where(l2 == m2, lane, float(LANE)), axis=-1, keepdims=True)
    e2 = jnp.exp(m2 - m1)
    w1 = 1.0 / (1.0 + e2)
    w2 = e2 / (1.0 + e2)
    route_ref[...] = jnp.where(lane == 0, i1, jnp.where(lane == 1, i2, jnp.where(lane == 2, w1, jnp.where(lane == 3, w2, 0.0))))


def norm_route(x, w, router):
    t, d = x.shape
    n_experts = router.shape[1]
    tm = _row_tile(t, 688)
    r_pad = jnp.zeros((d, LANE), F32).at[:, :n_experts].set(router)
    return pl.pallas_call(
        functools.partial(_norm_route_kernel, n_experts=n_experts),
        grid=(t // tm,),
        in_specs=[pl.BlockSpec((tm, d), lambda i: (i, 0)), pl.BlockSpec((1, d), lambda i: (0, 0)),
                  pl.BlockSpec((d, LANE), lambda i: (0, 0))],
        out_specs=[pl.BlockSpec((tm, d), lambda i: (i, 0)), pl.BlockSpec((tm, LANE), lambda i: (i, 0))],
        out_shape=[jax.ShapeDtypeStruct((t, d), BF16), jax.ShapeDtypeStruct((t, LANE), F32)],
        compiler_params=_cp(("parallel",)),
        name="norm_route",
    )(x, w.reshape(1, d), r_pad)


def _mm_kernel(x_ref, w_ref, o_ref):
    o_ref[...] = _dotf(x_ref[...], w_ref[...]).astype(o_ref.dtype)


def _mm_res_kernel(x_ref, w_ref, r_ref, o_ref):
    o_ref[...] = (r_ref[...] + _dotf(x_ref[...], w_ref[...])).astype(o_ref.dtype)


def matmul(x, w, resid=None, out_dtype=F32):
    t, k = x.shape
    n = w.shape[1]
    out_bytes = 4 * (3 if resid is None else 5)

    def fits(tm, tn):
        return 4 * k * (tm + tn) + out_bytes * tm * tn <= MATMUL_VMEM_BUDGET

    tm, tn = max(((a, b) for a in (_row_tile(t, c) for c in (1376, 688, 384, 128)) for b in (_col_tile(n, c) for c in (1024, 512, 256, 128))
                  if fits(a, b)), key=lambda ab: ab[0] * ab[1])
    in_specs = [pl.BlockSpec((tm, k), lambda j, i: (i, 0)), pl.BlockSpec((k, tn), lambda j, i: (0, j))]
    args = [x, w]
    kern = _mm_kernel
    if resid is not None:
        in_specs.append(pl.BlockSpec((tm, tn), lambda j, i: (i, j)))
        args.append(resid)
        kern = _mm_res_kernel
    return pl.pallas_call(
        kern,
        grid=(n // tn, t // tm),
        in_specs=in_specs,
        out_specs=pl.BlockSpec((tm, tn), lambda j, i: (i, j)),
        out_shape=jax.ShapeDtypeStruct((t, n), out_dtype),
        compiler_params=_cp(("parallel", "parallel")),
        name="matmul",
    )(*args)


def _mm_heads_kernel(x_ref, w_ref, o_ref, o3_ref):
    y = _dotf(x_ref[...], w_ref[...])
    o_ref[...] = y.astype(o_ref.dtype)
    for h in range(o3_ref.shape[1]):
        o3_ref[:, h, :] = y[:, h * LANE:(h + 1) * LANE]


def matmul_heads(x, w, row0, rows):
    k = x.shape[1]
    n = w.shape[1]
    tm = _row_tile(rows, 1024)
    assert row0 % tm == 0
    rb0 = row0 // tm
    tn = _col_tile(n, SUBLANE * LANE)
    return pl.pallas_call(
        _mm_heads_kernel,
        grid=(n // tn, rows // tm),
        in_specs=[pl.BlockSpec((tm, k), lambda j, i: (rb0 + i, 0)), pl.BlockSpec((k, tn), lambda j, i: (0, j))],
        out_specs=[pl.BlockSpec((tm, tn), lambda j, i: (i, j)),
                   pl.BlockSpec((tm, tn // LANE, LANE), lambda j, i: (i, j, 0))],
        out_shape=[jax.ShapeDtypeStruct((rows, n), BF16), jax.ShapeDtypeStruct((rows, n // LANE, LANE), F32)],
        compiler_params=_cp(("parallel", "parallel")),
        name="matmul_heads",
    )(x, w)


def _swiglu_step(x_ref, wg_ref, wu_ref, wd_ref):
    x = x_ref[...]
    g = _dotf(x, wg_ref[...].astype(BF16))
    u = _dotf(x, wu_ref[...].astype(BF16))
    a = (g * _sigmoid(g) * u).astype(BF16)
    return _dotf(a, wd_ref[...].astype(BF16))


def _ffn_kernel(x_ref, wg_ref, wu_ref, wd_ref, r_ref, o_ref):
    f = pl.program_id(1)
    y = _swiglu_step(x_ref, wg_ref, wu_ref, wd_ref)

    @pl.when(f == 0)
    def _():
        o_ref[...] = r_ref[...] + y

    @pl.when(f > 0)
    def _():
        o_ref[...] += y


def ffn(h, wg, wu, wd, resid):
    t, d = h.shape
    dff = wg.shape[1]
    tm = _row_tile(t, 688)
    tf = _col_tile(dff, 512)
    return pl.pallas_call(
        _ffn_kernel,
        grid=(t // tm, dff // tf),
        in_specs=[pl.BlockSpec((tm, d), lambda i, f: (i, 0)),
                  pl.BlockSpec((d, tf), lambda i, f: (0, f)),
                  pl.BlockSpec((d, tf), lambda i, f: (0, f)),
                  pl.BlockSpec((tf, d), lambda i, f: (f, 0)),
                  pl.BlockSpec((tm, d), lambda i, f: (i, 0))],
        out_specs=pl.BlockSpec((tm, d), lambda i, f: (i, 0)),
        out_shape=jax.ShapeDtypeStruct((t, d), F32),
        compiler_params=_cp(("parallel", "arbitrary")),
        name="ffn",
    )(h, wg, wu, wd, resid)


def _moe_ffn_kernel(te_ref, nu_ref, x_ref, wg_ref, wu_ref, wd_ref, o_ref):
    i = pl.program_id(0)
    f = pl.program_id(1)

    @pl.when(i < nu_ref[0])
    def _():
        y = _swiglu_step(x_ref, wg_ref, wu_ref, wd_ref)

        @pl.when(f == 0)
        def _():
            o_ref[...] = y

        @pl.when(f > 0)
        def _():
            o_ref[...] += y


def moe_ffn(xs, tile_expert, n_used, wg, wu, wd, layer):
    p, d = xs.shape
    dff = wg.shape[3]
    tm = MOE_TILE
    tf = _col_tile(dff, 256)
    nf = dff // tf

    def fidx(i, f, nu):
        return jnp.where(i < nu[0], f, nf - 1)

    grid_spec = pltpu.PrefetchScalarGridSpec(
        num_scalar_prefetch=2,
        grid=(p // tm, nf),
        in_specs=[pl.BlockSpec((tm, d), lambda i, f, te, nu: (jnp.minimum(i, nu[0] - 1), 0)),
                  pl.BlockSpec((None, None, d, tf), lambda i, f, te, nu: (layer, te[i], 0, fidx(i, f, nu))),
                  pl.BlockSpec((None, None, d, tf), lambda i, f, te, nu: (layer, te[i], 0, fidx(i, f, nu))),
                  pl.BlockSpec((None, None, tf, d), lambda i, f, te, nu: (layer, te[i], fidx(i, f, nu), 0))],
        out_specs=pl.BlockSpec((tm, d), lambda i, f, te, nu: (i, 0)),
    )
    return pl.pallas_call(
        _moe_ffn_kernel,
        grid_spec=grid_spec,
        out_shape=jax.ShapeDtypeStruct((p, d), F32),
        compiler_params=_cp(("arbitrary", "arbitrary")),
        name="moe_ffn",
    )(tile_expert, n_used, xs, wg, wu, wd)


def _gdn_gate_kernel(ba_ref, alog_ref, dtb_ref, ltri_ref, exp_ref, scal_ref, dec_ref, *, group, chunk):
    g4 = group
    ba = ba_ref[...]
    tm, width = ba.shape
    lane = lax.broadcasted_iota(I32, ba.shape, 1) % LANE
    is_b = lane < g4
    is_a = jnp.logical_and(lane >= g4, lane < 2 * g4)
    beta = _sigmoid(ba)
    xa = ba + dtb_ref[...]
    softplus = jnp.maximum(xa, 0.0) + jnp.log1p(jnp.exp(-jnp.abs(xa)))
    g = jnp.where(is_a, -jnp.exp(alog_ref[...]) * softplus, 0.0)
    gc = _dot_exact_lhs(ltri_ref[...], g)
    gc3 = gc.reshape(tm // chunk, chunk, width)
    gl = jnp.broadcast_to(gc3[:, chunk - 1:chunk, :], gc3.shape).reshape(tm, width)
    scal = (jnp.where(is_b, beta, 0.0) + jnp.where(is_a, gc, 0.0)
            + pltpu.roll(jnp.where(is_a, gl, 0.0), g4, 1)
            + pltpu.roll(jnp.where(is_a, gl - gc, 0.0), 2 * g4, 1))
    scal_ref[...] = scal

    dw = g4 * DEC_LANES
    row = lax.broadcasted_iota(I32, (tm, dw), 0)
    lane_h = lax.broadcasted_iota(I32, (tm, dw), 1) % DEC_LANES
    i_idx = row % chunk
    j_idx = lane_h % chunk
    same = ((row // chunk) % GDN_BATCH) == (lane_h // chunk)
    on_diag = jnp.logical_and(same, i_idx == j_idx)
    lower = jnp.logical_and(same, i_idx >= j_idx)
    for nb in range(width // LANE):
        gce = _dot_exact_rhs(gc[:, nb * LANE:(nb + 1) * LANE], exp_ref[...])
        diag = jnp.where(on_diag, gce, 0.0).reshape(tm // chunk, chunk, dw)
        gcj = jnp.broadcast_to(jnp.sum(diag, axis=1, keepdims=True), diag.shape).reshape(tm, dw)
        dec = jnp.where(lower, jnp.exp(gce - gcj), 0.0)
        dec_ref[:, nb * dw:(nb + 1) * dw] = dec.astype(dec_ref.dtype)


def gdn_gates(ba, alog2, dtb2, chunk):
    rows, width = ba.shape
    tm = _row_tile(rows, 512, chunk)
    ng = width // LANE
    g4 = GDN_GROUP
    r = jnp.arange(tm)
    ltri = jnp.logical_and(r[:, None] >= r[None, :], (r[:, None] // chunk) == (r[None, :] // chunk)).astype(BF16)
    src = jnp.arange(LANE)[:, None]
    dst = jnp.arange(g4 * DEC_LANES)[None, :]
    expand = (src == g4 + dst // DEC_LANES).astype(BF16)
    return pl.pallas_call(
        functools.partial(_gdn_gate_kernel, group=g4, chunk=chunk),
        grid=(rows // tm,),
        in_specs=[pl.BlockSpec((tm, width), lambda i: (i, 0)),
                  pl.BlockSpec((1, width), lambda i: (0, 0)),
                  pl.BlockSpec((1, width), lambda i: (0, 0)),
                  pl.BlockSpec((tm, tm), lambda i: (0, 0)),
                  pl.BlockSpec((LANE, g4 * DEC_LANES), lambda i: (0, 0))],
        out_specs=[pl.BlockSpec((tm, width), lambda i: (i, 0)),
                   pl.BlockSpec((tm, ng * g4 * DEC_LANES), lambda i: (i, 0))],
        out_shape=[jax.ShapeDtypeStruct((rows, width), F32),
                   jax.ShapeDtypeStruct((rows, ng * g4 * DEC_LANES), BF16)],
        compiler_params=_cp(("parallel",)),
        name="gdn_gates",
    )(ba, alog2, dtb2, ltri, expand)


def _gdn_conv_kernel(q_ref, k_ref, v_ref, qh_ref, kh_ref, vh_ref, qs_ref, ks_ref, vs_ref,
                     wq_ref, wk_ref, wv_ref, scal_ref,
                     qo_ref, ko_ref, kbo_ref, vbo_ref, kbgo_ref, qdo_ref, kdo_ref,
                     xq_ref, xk_ref, xv_ref, *, group, rep, tm, q_scale):
    first = pl.program_id(1) == 0

    def conv_silu(x_ref, halo_ref, hist_ref, w_ref, xp_ref):
        xp_ref[:SUBLANE, :] = jnp.where(first, hist_ref[...], halo_ref[...])
        xp_ref[SUBLANE:, :] = x_ref[...]
        w = w_ref[...]
        base = SUBLANE - (CONV_W - 1)
        y = xp_ref[SUBLANE:SUBLANE + tm, :] * w[CONV_W - 1:CONV_W]
        for j in range(CONV_W - 1):
            y = y + xp_ref[base + j:base + j + tm, :] * w[j:j + 1]
        return y * _sigmoid(y)

    def l2norm_heads(y):
        outs = []
        for h in range(y.shape[1] // LANE):
            yh = y[:, h * LANE:(h + 1) * LANE]
            outs.append(yh * lax.rsqrt(jnp.sum(yh * yh, axis=-1, keepdims=True) + EPS))
        return outs

    qn = [qh * q_scale for qh in l2norm_heads(conv_silu(q_ref, qh_ref, qs_ref, wq_ref, xq_ref))]
    kn = l2norm_heads(conv_silu(k_ref, kh_ref, ks_ref, wk_ref, xk_ref))
    v = conv_silu(v_ref, vh_ref, vs_ref, wv_ref, xv_ref)
    scal = scal_ref[...]
    for hq in range(group // rep):
        sl = slice(hq * LANE, (hq + 1) * LANE)
        qo_ref[:, sl] = qn[hq].astype(BF16)
        ko_ref[:, sl] = kn[hq].astype(BF16)
    for gh in range(group):
        hq = gh // rep
        sl = slice(gh * LANE, (gh + 1) * LANE)
        beta = scal[:, gh:gh + 1]
        eg = jnp.exp(scal[:, group + gh:group + gh + 1])
        ek = jnp.exp(scal[:, 3 * group + gh:3 * group + gh + 1])
        kb = kn[hq] * beta
        kbo_ref[:, sl] = kb.astype(BF16)
        vbo_ref[:, sl] = (v[:, sl] * beta).astype(BF16)
        kbgo_ref[:, sl] = (kb * eg).astype(BF16)
        qdo_ref[:, sl] = (qn[hq] * eg).astype(BF16)
        kdo_ref[:, sl] = (kn[hq] * ek).astype(BF16)


def gdn_conv(proj, row0, nseq, seqlen, hist, conv_w, scal, qk_dim, v_dim, dk):
    g4 = GDN_GROUP
    hv = v_dim // LANE
    rep = hv // (qk_dim // LANE)
    ng = hv // g4
    wqk = (g4 // rep) * LANE
    wv = g4 * LANE
    tm = _row_tile(seqlen, 256)
    nl = seqlen // tm
    rows = nseq * seqlen
    rb0 = row0 // tm
    nqk = qk_dim // wqk
    nv0 = 2 * qk_dim // wv
    cw = jnp.zeros((SUBLANE, conv_w.shape[1]), F32).at[:CONV_W].set(conv_w)

    def main(w, c0):
        return pl.BlockSpec((tm, w), lambda s, i, j: (rb0 + s * nl + i, c0 + j))

    def halo(w, c0):
        return pl.BlockSpec((SUBLANE, w), lambda s, i, j: (jnp.maximum((row0 + s * seqlen + i * tm) // SUBLANE - 1, 0), c0 + j))

    def hst(w, c0):
        return pl.BlockSpec((None, SUBLANE, w), lambda s, i, j: (s, 0, c0 + j))

    def wts(w, c0):
        return pl.BlockSpec((SUBLANE, w), lambda s, i, j: (0, c0 + j))

    def out(w):
        return pl.BlockSpec((tm, w), lambda s, i, j: (s * nl + i, j))

    return pl.pallas_call(
        functools.partial(_gdn_conv_kernel, group=g4, rep=rep, tm=tm, q_scale=float(dk) ** -0.5),
        grid=(nseq, nl, ng),
        in_specs=[main(wqk, 0), main(wqk, nqk), main(wv, nv0),
                  halo(wqk, 0), halo(wqk, nqk), halo(wv, nv0),
                  hst(wqk, 0), hst(wqk, nqk), hst(wv, nv0),
                  wts(wqk, 0), wts(wqk, nqk), wts(wv, nv0),
                  pl.BlockSpec((tm, LANE), lambda s, i, j: (s * nl + i, j))],
        out_specs=[out(wqk), out(wqk), out(wv), out(wv), out(wv), out(wv), out(wv)],
        out_shape=[jax.ShapeDtypeStruct((rows, qk_dim), BF16)] * 2 + [jax.ShapeDtypeStruct((rows, v_dim), BF16)] * 5,
        scratch_shapes=[pltpu.VMEM((SUBLANE + tm, wqk), F32), pltpu.VMEM((SUBLANE + tm, wqk), F32),
                        pltpu.VMEM((SUBLANE + tm, wv), F32)],
        compiler_params=_cp(("parallel", "parallel", "parallel")),
        name="gdn_conv",
    )(proj, proj, proj, proj, proj, proj, hist, hist, hist, cw, cw, cw, scal)


def _gdn_prep_kernel(q_ref, k_ref, kb_ref, vb_ref, kbg_ref, dec_ref, u_ref, w_ref, attn_ref, *, group, rep, chunk):
    nrow = q_ref.shape[0]
    ii = lax.broadcasted_iota(I32, (nrow, nrow), 0)
    jj = lax.broadcasted_iota(I32, (nrow, nrow), 1)
    strict = ii > jj
    eye = (ii == jj).astype(F32)
    if chunk < ATT_LANES:
        attn_ref[...] = jnp.zeros_like(attn_ref)
    kq_of = {}
    for hq in range(group // rep):
        slq = slice(hq * LANE, (hq + 1) * LANE)
        lhs = [kb_ref[:, (hq * rep + r) * LANE:(hq * rep + r + 1) * LANE] for r in range(rep)] + [q_ref[:, slq]]
        kq = _dot_nt(jnp.concatenate(lhs, axis=0), k_ref[:, slq])
        for r in range(rep):
            kq_of[hq * rep + r] = (kq[r * nrow:(r + 1) * nrow], kq[rep * nrow:])
    for gh in range(group):
        sl = slice(gh * LANE, (gh + 1) * LANE)
        dm = dec_ref[:, gh * DEC_LANES:gh * DEC_LANES + nrow].astype(F32)
        kk, qk = kq_of[gh]
        a = jnp.where(strict, kk * dm, 0.0)
        attn = (qk * dm).astype(BF16)
        for c in range(nrow // chunk):
            blk = slice(c * chunk, (c + 1) * chunk)
            attn_ref[blk, gh * ATT_LANES:gh * ATT_LANES + chunk] = attn[blk, blk]
        s = eye - a
        ab = a.astype(BF16)
        b = _dotf(ab, ab).astype(BF16)
        span = 2
        while span < chunk:
            if 2 * span < chunk:
                sb = _dotf(jnp.concatenate([s.astype(BF16), b], axis=0), b)
                s = s + sb[:nrow]
                b = sb[nrow:].astype(BF16)
            else:
                s = s + _dotf(s.astype(BF16), b)
            span *= 2
        sol = _dotf(s.astype(BF16), jnp.concatenate([vb_ref[:, sl], kbg_ref[:, sl]], axis=1))
        u_ref[:, sl] = sol[:, :LANE]
        w_ref[:, sl] = sol[:, LANE:].astype(BF16)


def gdn_prep(pre, dec, chunk):
    q, k, kb, vb, kbg, _, _ = pre
    g4 = GDN_GROUP
    rows, v_dim = kb.shape
    hv = v_dim // LANE
    rep = hv // (q.shape[1] // LANE)
    wqk = (g4 // rep) * LANE
    wv = g4 * LANE
    tm = GDN_BATCH * chunk
    assert rows % tm == 0

    def blk(w):
        return pl.BlockSpec((tm, w), lambda i, h: (i, h))

    return pl.pallas_call(
        functools.partial(_gdn_prep_kernel, group=g4, rep=rep, chunk=chunk),
        grid=(rows // tm, hv // g4),
        in_specs=[blk(wqk), blk(wqk), blk(wv), blk(wv), blk(wv), blk(g4 * DEC_LANES)],
        out_specs=[blk(wv), blk(wv), blk(g4 * ATT_LANES)],
        out_shape=[jax.ShapeDtypeStruct((rows, v_dim), F32), jax.ShapeDtypeStruct((rows, v_dim), BF16),
                   jax.ShapeDtypeStruct((rows, hv * ATT_LANES), BF16)],
        compiler_params=_cp(("parallel", "parallel")),
        name="gdn_prep",
    )(q, k, kb, vb, kbg, dec)


def _gdn_rec_kernel(u_ref, w_ref, qd_ref, kd_ref, attn_ref, scal_ref, z_ref, nw_ref, s0_ref, *rest,
                    group, lane_group, chunk, nchunks):
    o_ref, sout_ref, s_scr = rest[-3:]
    l = pl.program_id(2)

    @pl.when(l == 0)
    def _():
        s_scr[...] = s0_ref[...]

    nw = nw_ref[...]

    def one_chunk(c, carry):
        r0 = pl.multiple_of(c * chunk, chunk)
        rows = pl.ds(r0, chunk)
        sc = scal_ref[pl.ds(r0, SUBLANE), :]
        for gh in range(group):
            sl = slice(gh * LANE, (gh + 1) * LANE)
            s = s_scr[gh]
            sb = s.astype(BF16)
            ws = _dotf(jnp.concatenate([w_ref[rows, sl], qd_ref[rows, sl]], axis=0), sb)
            vnb = (u_ref[rows, sl] - ws[:chunk]).astype(BF16)
            kdt = kd_ref[rows, sl].astype(F32).T.astype(BF16)
            both = _dotf(jnp.concatenate([attn_ref[rows, gh * ATT_LANES:gh * ATT_LANES + chunk], kdt], axis=0), vnb)
            o = ws[chunk:] + both[:chunk]
            gl_lane = (gh // lane_group) * LANE + 2 * lane_group + gh % lane_group
            s_scr[gh] = s * jnp.exp(sc[0:1, gl_lane:gl_lane + 1]) + both[chunk:]
            ms = jnp.mean(o * o, axis=-1, keepdims=True)
            zz = z_ref[rows, sl]
            o_ref[rows, sl] = (o * lax.rsqrt(ms + EPS) * nw * (zz * _sigmoid(zz))).astype(o_ref.dtype)
        return carry

    lax.fori_loop(0, nchunks, one_chunk, 0)

    @pl.when(l == pl.num_programs(2) - 1)
    def _():
        sout_ref[...] = s_scr[...]


def gdn_core(pre, dec, scal, proj, row0, nseq, seqlen, chunk, s0, norm_w, z_col0, o_all, total_rows):
    u, w, attn = gdn_prep(pre, dec, chunk)
    qd, kd = pre[5], pre[6]
    g8 = GDN_REC_GROUP
    hv = qd.shape[1] // LANE
    wv = g8 * LANE
    lc = _row_tile(seqlen, 512, chunk)
    nl = seqlen // lc
    zc0 = z_col0 // wv
    rb0 = row0 // lc

    def loc(w_):
        return pl.BlockSpec((lc, w_), lambda s, h, l: (s * nl + l, h))

    def glob(w_, c0):
        return pl.BlockSpec((lc, w_), lambda s, h, l: (rb0 + s * nl + l, c0 + h))

    state = pl.BlockSpec((None, g8, LANE, LANE), lambda s, h, l: (s, h, 0, 0))
    in_specs = [loc(wv), loc(wv), loc(wv), loc(wv), loc(g8 * ATT_LANES), loc(g8 // GDN_GROUP * LANE), glob(wv, zc0),
                pl.BlockSpec((1, LANE), lambda s, h, l: (0, 0)), state]
    args = [u, w, qd, kd, attn, scal, proj, norm_w.reshape(1, LANE), s0]
    aliases = {}
    if o_all is not None:
        aliases = {len(args): 0}
        in_specs.append(pl.BlockSpec(memory_space=pl.ANY))
        args.append(o_all)
    o_new, s_out = pl.pallas_call(
        functools.partial(_gdn_rec_kernel, group=g8, lane_group=GDN_GROUP, chunk=chunk, nchunks=lc // chunk),
        grid=(nseq, hv // g8, nl),
        in_specs=in_specs,
        out_specs=[glob(wv, 0), state],
        out_shape=[jax.ShapeDtypeStruct((total_rows, hv * LANE), BF16), jax.ShapeDtypeStruct(s0.shape, F32)],
        scratch_shapes=[pltpu.VMEM((g8, LANE, LANE), F32)],
        input_output_aliases=aliases,
        compiler_params=_cp(("parallel", "parallel", "arbitrary")),
        name="gdn_rec",
    )(*args)
    return o_new, s_out


def _sb_block(qb, kb, vb, mfull, r, scale, diagonal):
    nq = qb.shape[0]
    nsb = kb.shape[0] // SB_SUB
    z = _dot_nt(qb, kb) * scale
    ls = jnp.minimum(z, 0.0) - jnp.log(1.0 + jnp.exp(-jnp.abs(z)))
    lr = ls - z
    if diagonal:
        keep = lax.broadcasted_iota(I32, z.shape, 1) < lax.broadcasted_iota(I32, z.shape, 0)
        lr = jnp.where(keep, lr, 0.0)
    hi = lr.astype(BF16)
    lo = (lr - hi.astype(F32)).astype(BF16)
    stacked = jnp.concatenate(
        [jnp.concatenate([hi[:, sb * SB_SUB:(sb + 1) * SB_SUB], lo[:, sb * SB_SUB:(sb + 1) * SB_SUB]], axis=1)
         for sb in range(nsb)], axis=0)
    cs = _dotf(stacked, mfull)
    after = [None] * nsb
    for sb in reversed(range(nsb)):
        part = cs[sb * nq:(sb + 1) * nq]
        after[sb] = part[:, :SB_SUB] + r
        r = r + part[:, SB_SUB:]
    wts = jnp.exp(ls + jnp.concatenate(after, axis=1))
    if diagonal:
        wts = jnp.where(keep, wts, 0.0)
    return _dotf(wts.astype(BF16), vb), r


def _sb_matrix():
    j = jnp.arange(2 * SB_SUB)[:, None] % SB_SUB
    s = jnp.arange(2 * SB_SUB)[None, :]
    return jnp.logical_or(s >= SB_SUB, j > s).astype(BF16)


def _sb_prompt_kernel(qi_ref, kj_ref, q_ref, k_ref, v_ref, m_ref, o_ref, acc, rsum, *, tile, scale, nh):
    p = pl.program_id(2)
    qi = qi_ref[p]
    kj = kj_ref[p]

    @pl.when(kj == qi)
    def _():
        acc[...] = jnp.zeros_like(acc)
        rsum[...] = jnp.zeros_like(rsum)

    def sweep(diagonal):
        mfull = m_ref[...]
        for h in range(nh):
            sl = slice(h * LANE, (h + 1) * LANE)
            y, r = _sb_block(q_ref[:, sl].astype(BF16), k_ref[:, sl].astype(BF16), v_ref[:, sl].astype(BF16),
                             mfull, rsum[h], scale, diagonal)
            acc[:, sl] += y
            rsum[h] = r

    @pl.when(kj == qi)
    def _():
        sweep(True)

    @pl.when(kj != qi)
    def _():
        sweep(False)

    @pl.when(kj == 0)
    def _():
        o_ref[...] = acc[...].astype(o_ref.dtype)


def sb_prompt(q, k, v, nseq, seqlen, heads, out_rows):
    tile = _row_tile(seqlen, SB_TILE, SB_SUB)
    nq = seqlen // tile
    nh = SB_HEADS
    ng = heads // nh
    wide = nh * LANE
    pairs = [(qi, kj) for qi in range(nq) for kj in range(qi, -1, -1)]
    qi_tab = jnp.asarray([a for a, _ in pairs], I32)
    kj_tab = jnp.asarray([b for _, b in pairs], I32)
    grid_spec = pltpu.PrefetchScalarGridSpec(
        num_scalar_prefetch=2,
        grid=(nseq, ng, len(pairs)),
        in_specs=[pl.BlockSpec((tile, wide), lambda b, h, p, qt, kt: (b * nq + qt[p], h)),
                  pl.BlockSpec((tile, wide), lambda b, h, p, qt, kt: (b * nq + kt[p], h)),
                  pl.BlockSpec((tile, wide), lambda b, h, p, qt, kt: (b * nq + kt[p], h)),
                  pl.BlockSpec((2 * SB_SUB, 2 * SB_SUB), lambda b, h, p, qt, kt: (0, 0))],
        out_specs=pl.BlockSpec((tile, wide), lambda b, h, p, qt, kt: (b * nq + qt[p], h)),
        scratch_shapes=[pltpu.VMEM((tile, wide), F32), pltpu.VMEM((nh, tile, LANE), F32)],
    )
    return pl.pallas_call(
        functools.partial(_sb_prompt_kernel, tile=tile, scale=float(LANE) ** -0.5, nh=nh),
        grid_spec=grid_spec,
        out_shape=jax.ShapeDtypeStruct((out_rows, heads * LANE), BF16),
        compiler_params=_cp(("parallel", "parallel", "arbitrary")),
        name="sb_prompt",
    )(qi_tab, kj_tab, q, k, v, _sb_matrix())


def _sb_decode_kernel(q_ref, kn_ref, vn_ref, kc_ref, vc_ref, m_ref, oprev_ref, o_ref, acc, rsum, *, heads, ls, tk, scale):
    del oprev_ref
    j = pl.program_id(1)
    mfull = m_ref[...]

    @pl.when(j == 0)
    def _():
        pad = jnp.zeros((SB_SUB - ls, LANE), BF16)
        for h in range(heads):
            sl = slice(h * LANE, (h + 1) * LANE)
            kn = jnp.concatenate([kn_ref[:, sl].astype(BF16), pad], axis=0)
            vn = jnp.concatenate([vn_ref[:, sl].astype(BF16), pad], axis=0)
            y, r = _sb_block(q_ref[:, sl].astype(BF16), kn, vn, mfull, jnp.zeros((ls, SB_SUB), F32), scale, True)
            acc[:, sl] = y
            rsum[h] = r

    @pl.when(j > 0)
    def _():
        for h in range(heads):
            sl = slice(h * LANE, (h + 1) * LANE)
            y, r = _sb_block(q_ref[:, sl].astype(BF16), kc_ref[:, h, :].astype(BF16), vc_ref[:, h, :].astype(BF16),
                             mfull, rsum[h], scale, False)
            acc[:, sl] += y
            rsum[h] = r

    @pl.when(j == pl.num_programs(1) - 1)
    def _():
        o_ref[...] = acc[...].astype(o_ref.dtype)


def sb_decode(q, k, v, row0, cache_k, cache_v, layer, o_all):
    _, nb, past, heads, hd = cache_k.shape
    width = heads * hd
    ls = (q.shape[0] - row0) // nb
    assert ls <= SB_SUB and ls % SUBLANE == 0
    tk = _row_tile(past, 512, SB_SUB)
    nkb = past // tk
    rb0 = row0 // ls

    def cache_idx(b, j):
        return (layer, b, jnp.where(j == 0, nkb - 1, nkb - j), 0, 0)

    return pl.pallas_call(
        functools.partial(_sb_decode_kernel, heads=heads, ls=ls, tk=tk, scale=float(LANE) ** -0.5),
        grid=(nb, nkb + 1),
        in_specs=[pl.BlockSpec((ls, width), lambda b, j: (rb0 + b, 0)),
                  pl.BlockSpec((ls, width), lambda b, j: (b, 0)),
                  pl.BlockSpec((ls, width), lambda b, j: (b, 0)),
                  pl.BlockSpec((None, None, tk, heads, hd), cache_idx),
                  pl.BlockSpec((None, None, tk, heads, hd), cache_idx),
                  pl.BlockSpec((2 * SB_SUB, 2 * SB_SUB), lambda b, j: (0, 0)),
                  pl.BlockSpec(memory_space=pl.ANY)],
        out_specs=pl.BlockSpec((ls, width), lambda b, j: (rb0 + b, 0)),
        out_shape=jax.ShapeDtypeStruct(o_all.shape, o_all.dtype),
        scratch_shapes=[pltpu.VMEM((ls, width), F32), pltpu.VMEM((heads, ls, SB_SUB), F32)],
        input_output_aliases={6: 0},
        compiler_params=_cp(("parallel", "arbitrary")),
        name="sb_decode",
    )(q, k, v, cache_k, cache_v, _sb_matrix(), o_all)


def _gdn_layer(x, tp, nseq_p, nseq_s, s_state, conv_state, norm_mix_w, w_in, conv_w, a_log, dt_bias, norm_w, w_out):
    t, d = x.shape
    _, hv, dk, dv = s_state.shape
    assert dk == LANE and dv == LANE
    qkv_dim = conv_state.shape[-1]
    v_dim = hv * dv
    qk_dim = (qkv_dim - v_dim) // 2
    g4 = GDN_GROUP
    ng = hv // g4
    lp = tp // nseq_p
    lsm = (t - tp) // nseq_s
    assert lp % CHUNK == 0 and lsm <= CHUNK and lsm >= CONV_W - 1

    h = rmsnorm(x, norm_mix_w, BF16)
    wide = qkv_dim + v_dim
    proj = matmul(h, w_in[:, :wide].astype(BF16))
    wb = w_in[:, wide:wide + hv].reshape(d, ng, g4)
    wa = w_in[:, wide + hv:wide + 2 * hv].reshape(d, ng, g4)
    w_ba = jnp.concatenate([wb, wa, jnp.zeros((d, ng, LANE - 2 * g4), F32)], axis=2).reshape(d, ng * LANE)
    ba = matmul(h, w_ba.astype(BF16))

    def lanes(vec):
        zero = jnp.zeros((ng, g4), F32)
        return jnp.concatenate([zero, vec.reshape(ng, g4), jnp.zeros((ng, LANE - 2 * g4), F32)], axis=1).reshape(1, ng * LANE)

    alog2, dtb2 = lanes(a_log), lanes(dt_bias)
    o_all = None
    states, convs = [], []
    parts = ((0, nseq_p, lp, CHUNK, jnp.zeros((nseq_p,) + s_state.shape[1:], F32), jnp.zeros((nseq_p, SUBLANE, qkv_dim), F32)),
             (tp, nseq_s, lsm, lsm, s_state,
              jnp.concatenate([jnp.zeros((nseq_s, SUBLANE - (CONV_W - 1), qkv_dim), F32), conv_state], axis=1)))
    for row0, nseq, seqlen, chunk, s0, hist in parts:
        rows = nseq * seqlen
        scal, dec = gdn_gates(lax.slice_in_dim(ba, row0, row0 + rows, axis=0), alog2, dtb2, chunk)
        pre = gdn_conv(proj, row0, nseq, seqlen, hist, conv_w, scal, qk_dim, v_dim, dk)
        o_all, s_new = gdn_core(pre, dec, scal, proj, row0, nseq, seqlen, chunk, s0, norm_w, qkv_dim, o_all, t)
        states.append(s_new)
        tails = [lax.slice(proj, (row0 + (s + 1) * seqlen - (CONV_W - 1), 0), (row0 + (s + 1) * seqlen, qkv_dim))
                 for s in range(nseq)]
        convs.append(jnp.stack(tails))
    x = matmul(o_all, w_out.astype(BF16), resid=x)
    return x, states, convs


def _sb_layer(x, tp, nseq_p, nseq_s, cache_k, cache_v, layer, norm_mix_w, w_qkv, w_out):
    t, d = x.shape
    _, nb, past, heads, hd = cache_k.shape
    assert hd == LANE and nb == nseq_s
    width = heads * hd
    h = rmsnorm(x, norm_mix_w, BF16)
    wb = w_qkv.astype(BF16)
    q = matmul(h, wb[:, :width], out_dtype=BF16)
    wk, wv = wb[:, width:2 * width], wb[:, 2 * width:]
    kp, kp_new = matmul_heads(h, wk, 0, tp)
    vp, vp_new = matmul_heads(h, wv, 0, tp)
    ksm, ks_new = matmul_heads(h, wk, tp, t - tp)
    vsm, vs_new = matmul_heads(h, wv, tp, t - tp)
    o = sb_prompt(q, kp, vp, nseq_p, tp // nseq_p, heads, t)
    o = sb_decode(q, ksm, vsm, tp, cache_k, cache_v, layer, o)
    x = matmul(o, w_out.astype(BF16), resid=x)
    lp, lsm = tp // nseq_p, (t - tp) // nseq_s
    ks = (kp_new.reshape(nseq_p, lp, heads, hd), ks_new.reshape(nseq_s, lsm, heads, hd))
    vs = (vp_new.reshape(nseq_p, lp, heads, hd), vs_new.reshape(nseq_s, lsm, heads, hd))
    return x, ks, vs


def _moe_layer(x, norm_w, router, wg, wu, wd, layer):
    t, d = x.shape
    n_experts = router.shape[1]
    h, route = norm_route(x, norm_w, router)
    expert = route[:, :TOP_K].astype(I32).reshape(-1)
    gate = route[:, TOP_K:2 * TOP_K]
    nslots = t * TOP_K
    onehot = (expert[:, None] == jnp.arange(n_experts, dtype=I32)[None, :]).astype(I32)
    counts = jnp.sum(onehot, axis=0)
    padded = (counts + MOE_TILE - 1) // MOE_TILE * MOE_TILE
    ends = jnp.cumsum(padded)
    starts = ends - padded
    rank = jnp.sum((jnp.cumsum(onehot, axis=0) - onehot) * onehot, axis=1)
    pos = starts[expert] + rank
    ntiles = (nslots + n_experts * (MOE_TILE - 1) + MOE_TILE - 1) // MOE_TILE
    tile_expert = jnp.minimum(jnp.searchsorted(ends, jnp.arange(ntiles, dtype=I32) * MOE_TILE, side="right"), n_experts - 1).astype(I32)
    n_used = (ends[-1] // MOE_TILE).astype(I32).reshape(1)
    row_token = jnp.zeros((ntiles * MOE_TILE,), I32).at[pos].set(jnp.arange(nslots, dtype=I32) // TOP_K)
    xs = jnp.take(h, row_token, axis=0)
    ys = moe_ffn(xs, tile_expert, n_used, wg, wu, wd, layer)
    pos = pos.reshape(t, TOP_K)
    for kk in range(TOP_K):
        x = x + jnp.take(ys, pos[:, kk], axis=0) * gate[:, kk:kk + 1]
    return x


def kernel(x_prompt, x_sample, state_gdn_S, state_gdn_conv, cache_sb_k, cache_sb_v, norm_mix, norm_ffn, norm_final, gdn_w_in, gdn_conv_w, gdn_a_log, gdn_dt_bias, gdn_norm_w, gdn_w_out, sb_w_qkv, sb_w_out, ffn_w_gate, ffn_w_up, ffn_w_down, moe_router, moe_w_gate, moe_w_up, moe_w_down):
    bp, lp, d = x_prompt.shape
    bs, lsm, _ = x_sample.shape
    tp = bp * lp
    x = jnp.concatenate([x_prompt.reshape(tp, d), x_sample.reshape(bs * lsm, d)], axis=0)
    depth = norm_mix.shape[0]
    s_p, s_s, c_p, c_s, k_p, k_s, v_p, v_s = [], [], [], [], [], [], [], []
    for i in range(depth):
        j = i // 2
        if i % 2 == 0:
            x, states, convs = _gdn_layer(x, tp, bp, bs, state_gdn_S[j], state_gdn_conv[j], norm_mix[i], gdn_w_in[j],
                                          gdn_conv_w[j], gdn_a_log[j], gdn_dt_bias[j], gdn_norm_w[j], gdn_w_out[j])
            s_p.append(states[0]); s_s.append(states[1]); c_p.append(convs[0]); c_s.append(convs[1])
            h = rmsnorm(x, norm_ffn[i], BF16)
            x = ffn(h, ffn_w_gate[j].astype(BF16), ffn_w_up[j].astype(BF16), ffn_w_down[j].astype(BF16), x)
        else:
            x, ks, vs = _sb_layer(x, tp, bp, bs, cache_sb_k, cache_sb_v, j, norm_mix[i], sb_w_qkv[j], sb_w_out[j])
            k_p.append(ks[0]); k_s.append(ks[1]); v_p.append(vs[0]); v_s.append(vs[1])
            x = _moe_layer(x, norm_ffn[i], moe_router[j], moe_w_gate, moe_w_up, moe_w_down, j)
    y = rmsnorm(x, norm_final, F32)
    return (y[:tp].reshape(bp, lp, d), y[tp:].reshape(bs, lsm, d),
            jnp.stack(s_p), jnp.stack(c_p), jnp.stack(k_p), jnp.stack(v_p),
            jnp.stack(s_s), jnp.stack(c_s), jnp.stack(k_s), jnp.stack(v_s))
```
